```python
import math
import jax
import jax.numpy as jnp
from jax import lax
import numpy as np

D_MODEL = 1024
BATCH = 4
SEQ = 8192
DEPTH = 4

CTX_LEN = 256
GRID_W = 64
N_MIXERS = 3
N_LAYERS_A = (DEPTH + 2) // 3
N_LAYERS_B = (DEPTH + 1) // 3
N_LAYERS_C = DEPTH // 3
N_MOD = 9
D_FF = 2816
EPS = 1e-6

HG_HEADS = 8
HG_DK = 128
HG_FDIM = HG_HEADS * HG_DK
HG_DV = D_MODEL // HG_HEADS
HG_IN = 3 * HG_FDIM + 2 * D_MODEL
HG_CHUNK = 64

POOL_WINDOWS = (2, 4, 8, 16)
POOL_GROUPS = len(POOL_WINDOWS)
POOL_GC = D_MODEL // POOL_GROUPS

HY_EMB = 33
HY_BANDS = (HY_EMB - 1) // 2
HY_ORDER = 64
HY_SHORT = 3
HY_TARGET = 1e-2
HY_FAST_PCT = 0.3
HY_SLOW_PCT = 1.5

kernel_name = 'hybrid_hgrn2_pool_hyena_dit'


def rms_norm(x):
    xf = x.astype(jnp.float32)
    return (xf * lax.rsqrt(jnp.mean(xf * xf, axis=-1, keepdims=True) + EPS)).astype(x.dtype)


def modulate(h, shift, scale):
    return rms_norm(h) * (1 + scale) + shift


def swiglu(y, w_gate, w_up, w_down):
    return (jax.nn.silu(y @ w_gate) * (y @ w_up)) @ w_down


def _chunk_gla(q, k, v, log_f, s0):
    bsz, nh, L, _ = q.shape
    dv = v.shape[-1]
    n_chunks = L // HG_CHUNK

    def to_chunks(a):
        return jnp.moveaxis(a.reshape(bsz, nh, n_chunks, HG_CHUNK, a.shape[-1]), 2, 0)

    lower = jnp.tril(jnp.ones((HG_CHUNK, HG_CHUNK), dtype=bool))[:, :, None]

    def step(state, inp):
        qc, kc, vc, gc = inp
        b = jnp.cumsum(gc, axis=2)
        diff = b[:, :, :, None, :] - b[:, :, None, :, :]
        decay = jnp.where(lower, jnp.exp(jnp.where(lower, diff, 0.0)), 0.0)
        scores = jnp.einsum('bhtd,bhsd,bhtsd->bhts', qc, kc, decay)
        o = (jnp.einsum('bhts,bhse->bhte', scores, vc)
             + jnp.einsum('bhtd,bhde->bhte', qc * jnp.exp(b), state))
        b_last = b[:, :, -1:, :]
        new_state = (jnp.exp(b_last[:, :, 0, :, None]) * state
                     + jnp.einsum('bhsd,bhse->bhde', kc * jnp.exp(b_last - b), vc))
        return new_state, o

    s_fin, o = lax.scan(step, s0, (to_chunks(q), to_chunks(k), to_chunks(v), to_chunks(log_f)))
    return jnp.moveaxis(o, 0, 2).reshape(bsz, nh, L, dv), s_fin


def hgrn2_mixer(u_lat, u_ctx, lower_bound, w_in, norm_gain, w_out, ctx_out):
    lb = lower_bound.reshape(2, 1, HG_HEADS, 1, HG_DK)

    def features(u):
        bsz, L, _ = u.shape
        q, z_f, z_b, inp, gate = jnp.split(
            u @ w_in, [HG_FDIM, 2 * HG_FDIM, 3 * HG_FDIM, 3 * HG_FDIM + D_MODEL], axis=-1)
        heads = lambda a: a.reshape(bsz, L, HG_HEADS, -1).transpose(0, 2, 1, 3).astype(jnp.float32)
        z = jnp.stack([heads(z_f), heads(z_b)])
        f = lb + (1 - lb) * jax.nn.sigmoid(z)
        log_f = jnp.log(f)
        k = (1 - lb) * jax.nn.sigmoid(-z)
        return heads(jax.nn.silu(q)), k, heads(inp), log_f, gate

    q_c, k_c, v_c, g_c, gate_c = features(u_ctx)
    q_l, k_l, v_l, g_l, gate_l = features(u_lat)
    s0 = jnp.zeros(q_c.shape[:2] + (HG_DK, HG_DV), jnp.float32)
    rev = lambda a: jnp.flip(a, axis=-2)

    o_c_f, s_f = _chunk_gla(q_c, k_c[0], v_c, g_c[0], s0)
    o_c_b, s_b = _chunk_gla(rev(q_c), rev(k_c[1]), rev(v_c), rev(g_c[1]), s0)
    o_l_f, _ = _chunk_gla(q_l, k_l[0], v_l, g_l[0], s_f)
    o_l_b, _ = _chunk_gla(rev(q_l), rev(k_l[1]), rev(v_l), rev(g_l[1]), s_b)

    def readout(o, gate, u):
        bsz, _, L, _ = o.shape
        o = rms_norm(o).transpose(0, 2, 1, 3).reshape(bsz, L, D_MODEL).astype(u.dtype)
        return (o * norm_gain * jax.nn.silu(gate)) @ w_out

    y_lat = readout(o_l_f + rev(o_l_b), gate_l, u_lat)
    y_ctx = readout(o_c_f + rev(o_c_b), gate_c, u_ctx) if ctx_out else None
    return y_lat, y_ctx


def _window_bounds(n, w):
    pos = jnp.arange(n)
    return jnp.clip(pos - w // 2, 0, n), jnp.clip(pos + (w - w // 2), 0, n)


def _box_mean_1d(g, w):
    L = g.shape[1]
    cs = jnp.pad(jnp.cumsum(g, axis=1), ((0, 0), (1, 0), (0, 0)))
    lo, hi = _window_bounds(L, w)
    return (cs[:, hi] - cs[:, lo]) / (hi - lo).astype(jnp.float32)[None, :, None]


def _box_mean_2d(g, w):
    R, W = g.shape[1], g.shape[2]
    sat = jnp.pad(jnp.cumsum(jnp.cumsum(g, axis=1), axis=2), ((0, 0), (1, 0), (1, 0), (0, 0)))
    rlo, rhi = _window_bounds(R, w)
    clo, chi = _window_bounds(W, w)
    s = (sat[:, rhi[:, None], chi[None, :]] - sat[:, rlo[:, None], chi[None, :]]
         - sat[:, rhi[:, None], clo[None, :]] + sat[:, rlo[:, None], clo[None, :]])
    cnt = ((rhi - rlo)[:, None] * (chi - clo)[None, :]).astype(jnp.float32)
    return s / cnt[None, :, :, None]


def pool_mixer(u, on_grid, w_pool, scale):
    bsz, L, _ = u.shape
    uf = u.astype(jnp.float32)
    outs = []
    for gi, w in enumerate(POOL_WINDOWS):
        g = uf[..., gi * POOL_GC:(gi + 1) * POOL_GC]
        if on_grid:
            rows = L // GRID_W
            m = _box_mean_2d(g.reshape(bsz, rows, GRID_W, POOL_GC), w).reshape(bsz, L, POOL_GC)
        else:
            m = _box_mean_1d(g, w)
        outs.append(m - g)
    pooled = jnp.stack(outs, axis=2).astype(u.dtype)
    y = jnp.einsum('blgc,gce->blge', pooled, w_pool).reshape(bsz, L, D_MODEL)
    return y * scale


def _hyena_filter_freq(L, w1, b1, w2, b2, w3, b3, w4, sin_freq):
    pos = jnp.arange(L, dtype=jnp.float32)
    t = jnp.linspace(0.0, 1.0, L, dtype=jnp.float32)
    ang = (2 * math.pi * pos / L)[:, None] * jnp.linspace(1e-4, HY_BANDS - 1, HY_BANDS, dtype=jnp.float32)[None, :]
    feats = jnp.concatenate([t[:, None], jnp.cos(ang), -jnp.sin(ang)], axis=1)
    h = jnp.sin(sin_freq * (feats @ w1 + b1))
    h = jnp.sin(sin_freq * (h @ w2 + b2))
    h = jnp.sin(sin_freq * (h @ w3 + b3))
    h = (h @ w4).astype(jnp.float32).reshape(L, 2, D_MODEL)
    deltas = jnp.abs(jnp.linspace(math.log(HY_TARGET) / HY_SLOW_PCT, math.log(HY_TARGET) / HY_FAST_PCT,
                                  D_MODEL, dtype=jnp.float32))
    h = h * jnp.exp(-t[:, None, None] * deltas)
    k_circ = jnp.concatenate([h[:, 0], jnp.zeros((1, D_MODEL), jnp.float32), h[:0:-1, 1]], axis=0)
    return jnp.fft.rfft(k_circ, axis=0)


def hyena_mixer(u, w_in, b_in, conv_w, conv_b, w1, b1, w2, b2, w3, b3, w4, sin_freq, filt_bias, w_out, b_out):
    L = u.shape[1]
    kf = _hyena_filter_freq(L, w1, b1, w2, b2, w3, b3, w4, sin_freq)
    z = u @ w_in + b_in
    zp = jnp.pad(z, ((0, 0), (1, 1), (0, 0)))
    z = zp[:, :-2] * conv_w[0] + zp[:, 1:-1] * conv_w[1] + zp[:, 2:] * conv_w[2] + conv_b
    x0, x1, v = jnp.split(z, 3, axis=-1)
    v = v * x1
    conv = jnp.fft.irfft(jnp.fft.rfft(v.astype(jnp.float32), n=2 * L, axis=1) * kf[None], n=2 * L, axis=1)[:, :L]
    y = x0 * (conv.astype(u.dtype) + v * filt_bias)
    return y @ w_out + b_out


def setup_inputs(seed: int = 0) -> dict:
    key = jax.random.key(seed)
    ks = iter(jax.random.split(key, 40))

    def nrm(shape, scale):
        return jax.random.normal(next(ks), shape, jnp.float32) * scale

    def ones_noise(shape):
        return 1.0 + nrm(shape, 0.1)

    D = D_MODEL
    return {
        'x': nrm((BATCH, SEQ, D), 1.0),
        'c': nrm((BATCH, D), 1.0),
        'ctx': nrm((BATCH, CTX_LEN, D), 1.0),
        'c_ctx': nrm((D,), 1.0),
        'w_ada': nrm((DEPTH, D, N_MOD * D), 0.5 * D ** -0.5),
        'b_ada': nrm((DEPTH, N_MOD * D), 0.02),
        'ffn_w_gate': nrm((DEPTH, 2, D, D_FF), D ** -0.5),
        'ffn_w_up': nrm((DEPTH, 2, D, D_FF), D ** -0.5),
        'ffn_w_down': nrm((DEPTH, 2, D_FF, D), D_FF ** -0.5),
        'hg_w_in': nrm((N_LAYERS_A, D, HG_IN), D ** -0.5),
        'hg_lb_logits': nrm((2, N_LAYERS_A, HG_FDIM), 0.5),
        'hg_norm_gain': ones_noise((N_LAYERS_A, D)),
        'hg_w_out': nrm((N_LAYERS_A, D, D), D ** -0.5),
        'pool_w': nrm((N_LAYERS_B, POOL_GROUPS, POOL_GC, POOL_GC), POOL_GC ** -0.5),
        'pool_scale': ones_noise((N_LAYERS_B, D)),
        'hy_w_in': nrm((N_LAYERS_C, D, 3 * D), D ** -0.5),
        'hy_b_in': nrm((N_LAYERS_C, 3 * D), 0.02),
        'hy_conv_w': nrm((N_LAYERS_C, HY_SHORT, 3 * D), HY_SHORT ** -0.5),
        'hy_conv_b': nrm((N_LAYERS_C, 3 * D), 0.02),
        'hy_w1': nrm((N_LAYERS_C, HY_EMB, HY_ORDER), HY_EMB ** -0.5),
        'hy_b1': nrm((N_LAYERS_C, HY_ORDER), 0.02),
        'hy_w2': nrm((N_LAYERS_C, HY_ORDER, HY_ORDER), HY_ORDER ** -0.5),
        'hy_b2': nrm((N_LAYERS_C, HY_ORDER), 0.02),
        'hy_w3': nrm((N_LAYERS_C, HY_ORDER, HY_ORDER), HY_ORDER ** -0.5),
        'hy_b3': nrm((N_LAYERS_C, HY_ORDER), 0.02),
        'hy_w4': nrm((N_LAYERS_C, HY_ORDER, 2 * D), 0.02 * HY_ORDER ** -0.5),
        'hy_sin_freq': ones_noise((N_LAYERS_C, HY_ORDER)),
        'hy_filt_bias': nrm((N_LAYERS_C, D), 0.5),
        'hy_w_out': nrm((N_LAYERS_C, D, D), D ** -0.5),
        'hy_b_out': nrm((N_LAYERS_C, D), 0.02),
        'final_gain': ones_noise((D,)),
    }


def reference(x, c, ctx, c_ctx, w_ada, b_ada, ffn_w_gate, ffn_w_up, ffn_w_down,
              hg_w_in, hg_lb_logits, hg_norm_gain, hg_w_out, pool_w, pool_scale,
              hy_w_in, hy_b_in, hy_conv_w, hy_conv_b, hy_w1, hy_b1, hy_w2, hy_b2, hy_w3, hy_b3,
              hy_w4, hy_sin_freq, hy_filt_bias, hy_w_out, hy_b_out, final_gain):
    p = jax.nn.softmax(hg_lb_logits.astype(jnp.float32), axis=1)
    lower_bounds = jnp.cumsum(p, axis=1) - p[:, :1]

    h, hc = x, ctx
    for i in range(DEPTH):
        kind, j = i % N_MIXERS, i // N_MIXERS
        last = i == DEPTH - 1
        ctx_live = (not last) or kind == 0
        mod = (jax.nn.silu(c) @ w_ada[i] + b_ada[i]).reshape(-1, N_MOD, 1, D_MODEL)
        h = h + 0.5 * mod[:, 2] * swiglu(modulate(h, mod[:, 0], mod[:, 1]),
                                         ffn_w_gate[i, 0], ffn_w_up[i, 0], ffn_w_down[i, 0])
        u = modulate(h, mod[:, 3], mod[:, 4])
        uc = None
        if ctx_live:
            mod_c = (jax.nn.silu(c_ctx) @ w_ada[i] + b_ada[i]).reshape(N_MOD, D_MODEL)
            hc = hc + 0.5 * mod_c[2] * swiglu(modulate(hc, mod_c[0], mod_c[1]),
                                              ffn_w_gate[i, 0], ffn_w_up[i, 0], ffn_w_down[i, 0])
            uc = modulate(hc, mod_c[3], mod_c[4])

        if kind == 0:
            y, yc = hgrn2_mixer(u, uc, lower_bounds[:, j], hg_w_in[j], hg_norm_gain[j], hg_w_out[j], not last)
        elif kind == 1:
            y = pool_mixer(u, True, pool_w[j], pool_scale[j])
            yc = None if last else pool_mixer(uc, False, pool_w[j], pool_scale[j])
        else:
            hy_args = (hy_w_in[j], hy_b_in[j], hy_conv_w[j], hy_conv_b[j], hy_w1[j], hy_b1[j], hy_w2[j],
                       hy_b2[j], hy_w3[j], hy_b3[j], hy_w4[j], hy_sin_freq[j], hy_filt_bias[j],
                       hy_w_out[j], hy_b_out[j])
            y = hyena_mixer(u, *hy_args)
            yc = None if last else hyena_mixer(uc, *hy_args)

        h = h + mod[:, 5] * y
        h = h + 0.5 * mod[:, 8] * swiglu(modulate(h, mod[:, 6], mod[:, 7]),
                                         ffn_w_gate[i, 1], ffn_w_up[i, 1], ffn_w_down[i, 1])
        if not last:
            hc = hc + mod_c[5] * yc
            hc = hc + 0.5 * mod_c[8] * swiglu(modulate(hc, mod_c[6], mod_c[7]),
                                              ffn_w_gate[i, 1], ffn_w_up[i, 1], ffn_w_down[i, 1])

    return rms_norm(h) * final_gain
```

```python
import functools
import math

import jax
import jax.numpy as jnp
import numpy as np
from jax import lax
from jax.experimental import pallas as pl
from jax.experimental.pallas import tpu as pltpu

F32 = jnp.float32
BF16 = jnp.bfloat16

D_MODEL = 1024
DEPTH = 4
N_MOD = 9
D_FF = 2816
EPS = 1e-6
GRID_W = 64

HG_HEADS = 8
HG_DK = 128

POOL_WINDOWS = (2, 4, 8, 16)
POOL_GC = D_MODEL // len(POOL_WINDOWS)

HY_EMB = 33
HY_BANDS = (HY_EMB - 1) // 2
HY_ORDER = 64
HY_TARGET = 1e-2
HY_FAST_PCT = 0.3
HY_SLOW_PCT = 1.5

MOD_ROWS = 8
V7X_VMEM_LIMIT = 56 * 1024 * 1024


def _params(n_grid, vmem=None):
    return pltpu.CompilerParams(dimension_semantics=("arbitrary",) * n_grid,
                                vmem_limit_bytes=vmem or V7X_VMEM_LIMIT)


def _silu(x):
    return x * jax.nn.sigmoid(x)


def _rms(x):
    return x * lax.rsqrt(jnp.mean(x * x, axis=-1, keepdims=True) + EPS)


def _modulate(x, shift, scale):
    return _rms(x) * (1.0 + scale) + shift


def _ada_kernel(cc_ref, w_ref, b_ref, o_ref):
    s = _silu(cc_ref[...]).astype(BF16)
    o_ref[...] = jnp.dot(s, w_ref[...].astype(BF16), preferred_element_type=F32) + b_ref[...]


def ada_table(c, c_ctx, w_ada, b_ada):
    bsz, d = c.shape
    depth, _, nd = w_ada.shape
    cc = jnp.zeros((MOD_ROWS, d), F32).at[:bsz].set(c).at[bsz].set(c_ctx)
    tn = 1024
    out = pl.pallas_call(
        _ada_kernel,
        out_shape=jax.ShapeDtypeStruct((depth, MOD_ROWS, nd), F32),
        grid=(depth, nd // tn),
        in_specs=[pl.BlockSpec((MOD_ROWS, d), lambda l, j: (0, 0)),
                  pl.BlockSpec((None, d, tn), lambda l, j: (l, 0, j)),
                  pl.BlockSpec((None, 1, tn), lambda l, j: (l, 0, j))],
        out_specs=pl.BlockSpec((None, MOD_ROWS, tn), lambda l, j: (l, 0, j)),
        compiler_params=_params(2),
        name="ada_table",
    )(cc, w_ada, b_ada.reshape(depth, 1, nd))
    return out.reshape(depth, MOD_ROWS, N_MOD, d)


FFN_CHUNK = 256
FFN_ROWS = 512


def _ffn_kernel(h_ref, mod_ref, wg_ref, wu_ref, wd_ref, *rest, k0, n_chunks, final):
    if final:
        gain_ref, o_ref, acc_ref = rest
    else:
        o_ref, acc_ref = rest
    x = h_ref[...]
    y = _modulate(x, mod_ref[k0:k0 + 1, :], mod_ref[k0 + 1:k0 + 2, :]).astype(BF16)
    acc_ref[...] = jnp.zeros_like(acc_ref)

    def body(j, carry):
        g = jnp.dot(y, wg_ref[j], preferred_element_type=F32)
        u = jnp.dot(y, wu_ref[j], preferred_element_type=F32)
        a = (_silu(g) * u).astype(BF16)
        acc_ref[...] += jnp.dot(a, wd_ref[j], preferred_element_type=F32)
        return carry

    lax.fori_loop(0, n_chunks, body, 0)
    out = x + (0.5 * mod_ref[k0 + 2:k0 + 3, :]) * acc_ref[...]
    if final:
        out = _rms(out) * gain_ref[...]
    o_ref[...] = out


def _chunk_cols(w):
    d, f = w.shape
    return w.astype(BF16).reshape(d, f // FFN_CHUNK, FFN_CHUNK).transpose(1, 0, 2)


def ffn_half_step(h, mod, is_ctx, k0, w_gate, w_up, w_down, final_gain=None):
    bsz, L, d = h.shape
    f = w_gate.shape[1]
    n_chunks = f // FFN_CHUNK
    tm = min(FFN_ROWS, L)
    wg = _chunk_cols(w_gate)
    wu = _chunk_cols(w_up)
    wd = w_down.astype(BF16).reshape(n_chunks, FFN_CHUNK, d)
    row = (lambda b, i: (bsz, 0, 0)) if is_ctx else (lambda b, i: (b, 0, 0))
    const3 = lambda b, i: (0, 0, 0)
    single = pl.Buffered(1)
    in_specs = [pl.BlockSpec((None, tm, d), lambda b, i: (b, i, 0)),
                pl.BlockSpec((None, N_MOD, d), row),
                pl.BlockSpec((n_chunks, d, FFN_CHUNK), const3, pipeline_mode=single),
                pl.BlockSpec((n_chunks, d, FFN_CHUNK), const3, pipeline_mode=single),
                pl.BlockSpec((n_chunks, FFN_CHUNK, d), const3, pipeline_mode=single)]
    args = [h, mod, wg, wu, wd]
    if final_gain is not None:
        in_specs.append(pl.BlockSpec((1, d), lambda b, i: (0, 0)))
        args.append(final_gain.reshape(1, d))
    return pl.pallas_call(
        functools.partial(_ffn_kernel, k0=k0, n_chunks=n_chunks, final=final_gain is not None),
        out_shape=jax.ShapeDtypeStruct((bsz, L, d), F32),
        grid=(bsz, L // tm),
        in_specs=in_specs,
        out_specs=pl.BlockSpec((None, tm, d), lambda b, i: (b, i, 0)),
        scratch_shapes=[pltpu.VMEM((tm, d), F32)],
        compiler_params=_params(2),
        name="ffn_half_step",
    )(*args)


HG_ROWS = 256
GLA_TILE = 256
GLA_BLOCK = 32
GLA_SAFE_LOG_DECAY = 60.0


def _split3(x):
    x1 = x.astype(BF16)
    r1 = x - x1.astype(F32)
    x2 = r1.astype(BF16)
    x3 = (r1 - x2.astype(F32)).astype(BF16)
    return x1, x2, x3


def _dot_exact_lhs(a_bf16, x):
    x1, x2, x3 = _split3(x)
    return (jnp.dot(a_bf16, x1, preferred_element_type=F32)
            + jnp.dot(a_bf16, x2, preferred_element_type=F32)
            + jnp.dot(a_bf16, x3, preferred_element_type=F32))


def _hg_proj_kernel(h_ref, mod_ref, w_ref, lbl_ref, q_ref, kf_ref, gf_ref, kb_ref, gb_ref, v_ref, sg_ref, *, layer_j):
    x = h_ref[...]
    u = _modulate(x, mod_ref[3:4, :], mod_ref[4:5, :]).astype(BF16)
    q_ref[...] = _silu(jnp.dot(u, w_ref[0], preferred_element_type=F32))
    v_ref[...] = jnp.dot(u, w_ref[3], preferred_element_type=F32)
    sg_ref[...] = _silu(jnp.dot(u, w_ref[4], preferred_element_type=F32))
    for d, (k_ref, g_ref) in enumerate(((kf_ref, gf_ref), (kb_ref, gb_ref))):
        logits = lbl_ref[d]
        e = jnp.exp(logits - jnp.max(logits, axis=0, keepdims=True))
        p = e / jnp.sum(e, axis=0, keepdims=True)
        lb = jnp.sum(p[:layer_j + 1], axis=0, keepdims=True) - p[0:1]
        z = jnp.dot(u, w_ref[1 + d], preferred_element_type=F32)
        g_ref[...] = jnp.log(lb + (1.0 - lb) * jax.nn.sigmoid(z))
        k_ref[...] = (1.0 - lb) * jax.nn.sigmoid(-z)


def hg_project(h, mod, is_ctx, w_in, lb_logits, layer_j):
    bsz, L, d = h.shape
    tm = min(HG_ROWS, L)
    n_blk = w_in.shape[1] // d
    w = w_in.astype(BF16).reshape(d, n_blk, d).transpose(1, 0, 2)
    row = (lambda b, i: (bsz, 0, 0)) if is_ctx else (lambda b, i: (b, 0, 0))
    tile = pl.BlockSpec((None, tm, d), lambda b, i: (b, i, 0))
    shp = jax.ShapeDtypeStruct((bsz, L, d), F32)
    return pl.pallas_call(
        functools.partial(_hg_proj_kernel, layer_j=layer_j),
        out_shape=(shp,) * 7,
        grid=(bsz, L // tm),
        in_specs=[tile,
                  pl.BlockSpec((None, N_MOD, d), row),
                  pl.BlockSpec((n_blk, d, d), lambda b, i: (0, 0, 0), pipeline_mode=pl.Buffered(1)),
                  pl.BlockSpec(lb_logits.shape, lambda b, i: (0, 0, 0))],
        out_specs=(tile,) * 7,
        compiler_params=_params(2),
        name="hg_project",
    )(h, mod, w, lb_logits)


def _gla_consts(tl, blk):
    r = np.arange(tl)
    same = (r[:, None] // blk) == (r[None, :] // blk)
    tri_f = same & (r[None, :] <= r[:, None])
    tri_b = same & (r[None, :] >= r[:, None])
    return (jnp.asarray(np.stack([tri_f, tri_b]), BF16), jnp.asarray(same, BF16))


def _gla_kernel(qf_ref, qb_ref, kf_ref, gf_ref, kb_ref, gb_ref, vf_ref, vb_ref, tri_ref, ones_ref, s0_ref,
                of_ref, ob_ref, sfin_ref,
                st_ref, bc_ref, qd_ref, kt_ref, ke_ref, vv_ref, dec_ref, *, n_blk, blk, heads, dk):
    i = pl.program_id(1)

    @pl.when(i == 0)
    def _():
        st_ref[...] = s0_ref[...]

    dirs = ((qf_ref, kf_ref, gf_ref, vf_ref, of_ref), (qb_ref, kb_ref, gb_ref, vb_ref, ob_ref))
    worst = None
    for d, (q_ref, k_ref, g_ref, v_ref, _) in enumerate(dirs):
        g = g_ref[...]
        bc = _dot_exact_lhs(tri_ref[d], g)
        bl = _dot_exact_lhs(ones_ref[...], g)
        bc_ref[d] = bc
        dec_ref[d] = jnp.exp(bl)
        qd_ref[d] = (q_ref[...] * jnp.exp(bc)).astype(BF16)
        ke_ref[d] = (k_ref[...] * jnp.exp(bl - bc)).astype(BF16)
        vv_ref[d] = v_ref[...].astype(BF16)
        m = jnp.min(bl)
        worst = m if worst is None else jnp.minimum(worst, m)
    safe = worst >= -GLA_SAFE_LOG_DECAY

    @pl.when(safe)
    def _():
        for d, (_, k_ref, _, _, _) in enumerate(dirs):
            kt_ref[d] = (k_ref[...] * jnp.exp(-bc_ref[d])).astype(BF16)

    t_io = lax.broadcasted_iota(jnp.int32, (blk, blk), 0)
    s_io = lax.broadcasted_iota(jnp.int32, (blk, blk), 1)
    t_io_w = lax.broadcasted_iota(jnp.int32, (blk, dk), 0)
    sub_io = lax.broadcasted_iota(jnp.int32, (8, dk), 0)
    nt =(((1,), (1,)), ((), ()))
    tn = (((0,), (0,)), ((), ()))

    def run(fast):
        def step(jb, carry):
            for d, (q_ref, k_ref, _, _, o_ref) in enumerate(dirs):
                r0 = pl.multiple_of((jb if d == 0 else n_blk - 1 - jb) * blk, blk)
                rows = pl.ds(r0, blk)
                keep = (s_io <= t_io) if d == 0 else (s_io >= t_io)
                for hh in range(heads):
                    cols = slice(hh * dk, (hh + 1) * dk)
                    qd = qd_ref[d, rows, cols]
                    if fast:
                        sc = lax.dot_general(qd, kt_ref[d, rows, cols], nt, preferred_element_type=F32)
                        sc = jnp.where(keep, sc, 0.0)
                    else:
                        qq = q_ref[rows, cols]
                        bb = bc_ref[d, rows, cols]

                        def col(s, sc):
                            grp = pl.ds(pl.multiple_of(r0 + (s // 8) * 8, 8), 8)
                            pick = sub_io == s % 8
                            brow = jnp.sum(jnp.where(pick, bc_ref[d, grp, cols], 0.0), axis=0, keepdims=True)
                            krow = jnp.sum(jnp.where(pick, k_ref[grp, cols], 0.0), axis=0, keepdims=True)
                            ok = (t_io_w >= s) if d == 0 else (t_io_w <= s)
                            p = jnp.where(ok, qq * krow * jnp.exp(jnp.minimum(bb - brow, 0.0)), 0.0)
                            return jnp.where(s_io == s, jnp.sum(p, axis=-1, keepdims=True), sc)

                        sc = lax.fori_loop(0, blk, col, jnp.zeros((blk, blk), F32))
                    st = st_ref[d, hh]
                    vv = vv_ref[d, rows, cols]
                    o = (jnp.dot(sc.astype(BF16), vv, preferred_element_type=F32)
                         + lax.dot_general(qd, st.astype(BF16), nt, preferred_element_type=F32))
                    o_ref[rows, cols] = o
                    dec = dec_ref[d, pl.ds(r0, 1), cols]
                    st_ref[d, hh] = st * dec + lax.dot_general(vv, ke_ref[d, rows, cols], tn,
                                                               preferred_element_type=F32)
            return carry

        lax.fori_loop(0, n_blk, step, 0)

    @pl.when(safe)
    def _():
        run(True)

    @pl.when(jnp.logical_not(safe))
    def _():
        run(False)

    @pl.when(i == pl.num_programs(1) - 1)
    def _():
        sfin_ref[...] = st_ref[...]


def gla_bidirectional(q, k_f, g_f, k_b, g_b, v, s0):
    bsz, L, d = q.shape
    tl = min(GLA_TILE, L)
    nt = L // tl
    n_blk = tl // GLA_BLOCK
    tri, ones = _gla_consts(tl, GLA_BLOCK)
    fwd = pl.BlockSpec((None, tl, d), lambda b, i: (b, i, 0))
    bwd = pl.BlockSpec((None, tl, d), lambda b, i: (b, nt - 1 - i, 0))
    st_spec = pl.BlockSpec((2, None, HG_HEADS, HG_DK, HG_DK), lambda b, i: (0, b, 0, 0, 0))
    shp = jax.ShapeDtypeStruct((bsz, L, d), F32)
    return pl.pallas_call(
        functools.partial(_gla_kernel, n_blk=n_blk, blk=GLA_BLOCK, heads=HG_HEADS, dk=HG_DK),
        out_shape=(shp, shp, jax.ShapeDtypeStruct(s0.shape, F32)),
        grid=(bsz, nt),
        in_specs=[fwd, bwd, fwd, fwd, bwd, bwd, fwd, bwd,
                  pl.BlockSpec((2, tl, tl), lambda b, i: (0, 0, 0)),
                  pl.BlockSpec((tl, tl), lambda b, i: (0, 0)),
                  st_spec],
        out_specs=(fwd, bwd, st_spec),
        scratch_shapes=[pltpu.VMEM((2, HG_HEADS, HG_DK, HG_DK), F32),
                        pltpu.VMEM((2, tl, d), F32),
                        pltpu.VMEM((2, tl, d), BF16),
                        pltpu.VMEM((2, tl, d), BF16),
                        pltpu.VMEM((2, tl, d), BF16),
                        pltpu.VMEM((2, tl, d), BF16),
                        pltpu.VMEM((2, tl, d), F32)],
        compiler_params=_params(2),
        name="gla_bidirectional",
    )(q, q, k_f, g_f, k_b, g_b, v, v, tri, ones, s0)


def _hg_readout_kernel(h_ref, of_ref, ob_ref, sg_ref, mod_ref, gain_ref, w_ref, o_ref, *, heads, dk):
    o = of_ref[...] + ob_ref[...]
    parts = []
    for hh in range(heads):
        parts.append(_rms(o[:, hh * dk:(hh + 1) * dk]))
    on = jnp.concatenate(parts, axis=-1)
    y = (on * gain_ref[...] * sg_ref[...]).astype(BF16)
    o_ref[...] = h_ref[...] + mod_ref[5:6, :] * jnp.dot(y, w_ref[...], preferred_element_type=F32)


def hg_readout(h, o_f, o_b, sg, mod, is_ctx, norm_gain, w_out):
    bsz, L, d = h.shape
    tm = min(HG_ROWS, L)
    row = (lambda b, i: (bsz, 0, 0)) if is_ctx else (lambda b, i: (b, 0, 0))
    tile = pl.BlockSpec((None, tm, d), lambda b, i: (b, i, 0))
    return pl.pallas_call(
        functools.partial(_hg_readout_kernel, heads=HG_HEADS, dk=HG_DK),
        out_shape=jax.ShapeDtypeStruct((bsz, L, d), F32),
        grid=(bsz, L // tm),
        in_specs=[tile, tile, tile, tile,
                  pl.BlockSpec((None, N_MOD, d), row),
                  pl.BlockSpec((1, d), lambda b, i: (0, 0)),
                  pl.BlockSpec((d, d), lambda b, i: (0, 0))],
        out_specs=tile,
        compiler_params=_params(2),
        name="hg_readout",
    )(h, o_f, o_b, sg, mod, norm_gain.reshape(1, d), w_out.astype(BF16))


def hgrn2_layer(h, hc, mod, w_in, lb_logits, layer_j, norm_gain, w_out, ctx_out):
    bsz = h.shape[0]
    qc, kfc, gfc, kbc, gbc, vc, sgc = hg_project(hc, mod, True, w_in, lb_logits, layer_j)
    ql, kfl, gfl, kbl, gbl, vl, sgl = hg_project(h, mod, False, w_in, lb_logits, layer_j)
    s0 = jnp.zeros((2, bsz, HG_HEADS, HG_DK, HG_DK), F32)
    ocf, ocb, s_ctx = gla_bidirectional(qc, kfc, gfc, kbc, gbc, vc, s0)
    olf, olb, _ = gla_bidirectional(ql, kfl, gfl, kbl, gbl, vl, s_ctx)
    h = hg_readout(h, olf, olb, sgl, mod, False, norm_gain, w_out)
    if ctx_out:
        hc = hg_readout(hc, ocf, ocb, sgc, mod, True, norm_gain, w_out)
    return h, hc


POOL_ROWS = 256


def _window_matrix(n, w, period=None):
    p = np.arange(n)
    period = period or n
    lo, hi = p - w // 2, p + (w - w // 2)
    same = (p[:, None] // period) == (p[None, :] // period)
    return same & (p[None, :] >= lo[:, None]) & (p[None, :] < hi[:, None])


def _window_count(pos, n, w):
    return (jnp.clip(pos + (w - w // 2), 0, n) - jnp.clip(pos - w // 2, 0, n)).astype(F32)


def _pool_row_kernel(h_ref, mod_ref, a_ref, o_ref, *, n_rows):
    u = _modulate(h_ref[...], mod_ref[3:4, :], mod_ref[4:5, :])
    r = lax.broadcasted_iota(jnp.int32, (n_rows, 1), 0)
    for gi, w in enumerate(POOL_WINDOWS):
        cols = slice(gi * POOL_GC, (gi + 1) * POOL_GC)
        o_ref[:, cols] = _dot_exact_lhs(a_ref[gi], u[:, cols]) / _window_count(r, n_rows, w)


def _pool_main_kernel(*refs, period, on_grid):
    if on_grid:
        h_ref, r_ref, mod_ref, a_ref, wp_ref, sc_ref, o_ref = refs
    else:
        h_ref, mod_ref, a_ref, wp_ref, sc_ref, o_ref = refs
    x = h_ref[...]
    u = _modulate(x, mod_ref[3:4, :], mod_ref[4:5, :])
    src = r_ref[...] if on_grid else u
    pos = lax.broadcasted_iota(jnp.int32, (x.shape[0], 1), 0) % period
    for gi, w in enumerate(POOL_WINDOWS):
        cols = slice(gi * POOL_GC, (gi + 1) * POOL_GC)
        mean = _dot_exact_lhs(a_ref[gi], src[:, cols]) / _window_count(pos, period, w)
        pooled = (mean - u[:, cols]).astype(BF16)
        y = jnp.dot(pooled, wp_ref[gi], preferred_element_type=F32) * sc_ref[:, cols]
        o_ref[:, cols] = x[:, cols] + mod_ref[5:6, cols] * y


def pool_layer(h, mod, is_ctx, w_pool, scale):
    bsz, L, d = h.shape
    wp = w_pool.astype(BF16)
    row = (lambda b, i: (bsz, 0, 0)) if is_ctx else (lambda b, i: (b, 0, 0))
    mod_spec = pl.BlockSpec((None, N_MOD, d), row)
    tm = min(POOL_ROWS, L)
    tile = pl.BlockSpec((None, tm, d), lambda b, i: (b, i, 0))
    ng = len(POOL_WINDOWS)
    args, specs = [h], [tile]
    if is_ctx:
        assert tm == L, "the 1D windows need the whole context in one tile"
        period = L
    else:
        n_rows = L // GRID_W
        a_row = jnp.asarray(np.stack([_window_matrix(n_rows, w) for w in POOL_WINDOWS]), BF16)
        rows = pl.pallas_call(
            functools.partial(_pool_row_kernel, n_rows=n_rows),
            out_shape=jax.ShapeDtypeStruct((bsz, n_rows, GRID_W * d), F32),
            grid=(bsz, GRID_W),
            in_specs=[pl.BlockSpec((None, n_rows, d), lambda b, c: (b, 0, c)),
                      pl.BlockSpec((None, N_MOD, d), lambda b, c: (b, 0, 0)),
                      pl.BlockSpec((ng, n_rows, n_rows), lambda b, c: (0, 0, 0))],
            out_specs=pl.BlockSpec((None, n_rows, d), lambda b, c: (b, 0, c)),
            compiler_params=_params(2),
            name="pool_rows",
        )(h.reshape(bsz, n_rows, GRID_W * d), mod, a_row)
        args.append(rows.reshape(bsz, L, d))
        specs.append(tile)
        period = GRID_W
    a_col = jnp.asarray(np.stack([_window_matrix(tm, w, period) for w in POOL_WINDOWS]), BF16)
    args += [mod, a_col, wp, scale.reshape(1, d)]
    specs += [mod_spec,
              pl.BlockSpec((ng, tm, tm), lambda b, i: (0, 0, 0)),
              pl.BlockSpec((ng, POOL_GC, POOL_GC), lambda b, i: (0, 0, 0)),
              pl.BlockSpec((1, d), lambda b, i: (0, 0))]
    return pl.pallas_call(
        functools.partial(_pool_main_kernel, period=period, on_grid=not is_ctx),
        out_shape=jax.ShapeDtypeStruct((bsz, L, d), F32),
        grid=(bsz, L // tm),
        in_specs=specs,
        out_specs=tile,
        compiler_params=_params(2),
        name="pool_main",
    )(*args)


HY_ROWS = 256
HY_HALO = 8
FFT_N2 = 128
FFT_LANES = 2048
FFT_CH = 512
FFT_DIRECT_MAX = 512


def _dot_f32(a, b):
    a1, a2, a3 = _split3(a)
    b1, b2, b3 = _split3(b)
    d = lambda x, y: jnp.dot(x, y, preferred_element_type=F32)
    return d(a1, b1) + (d(a1, b2) + d(a2, b1)) + (d(a1, b3) + d(a2, b2) + d(a3, b1))


def _hy_in_kernel(h_ref, hp_ref, hn_ref, mod_ref, w_ref, b_ref, cw_ref, cb_ref, x0_ref, vx_ref, *, tm):
    i = pl.program_id(1)
    last = pl.num_programs(1) - 1
    sh, sc = mod_ref[3:4, :], mod_ref[4:5, :]
    u = _modulate(h_ref[...], sh, sc).astype(BF16)
    up = _modulate(hp_ref[...], sh, sc).astype(BF16)
    un = _modulate(hn_ref[...], sh, sc).astype(BF16)
    row = lax.broadcasted_iota(jnp.int32, (tm, 1), 0)
    outs = []
    for blk in range(3):
        w, b = w_ref[blk], b_ref[blk]
        z = jnp.dot(u, w, preferred_element_type=F32) + b
        zp = jnp.dot(up, w, preferred_element_type=F32) + b
        zn = jnp.dot(un, w, preferred_element_type=F32) + b
        before = jnp.where(i == 0, 0.0, zp[HY_HALO - 1:HY_HALO, :])
        after = jnp.where(i == last, 0.0, zn[0:1, :])
        zm1 = jnp.where(row == 0, before, pltpu.roll(z, 1, axis=0))
        zp1 = jnp.where(row == tm - 1, after, pltpu.roll(z, tm - 1, axis=0))
        outs.append(zm1 * cw_ref[0, blk] + z * cw_ref[1, blk] + zp1 * cw_ref[2, blk] + cb_ref[blk])
    x0_ref[...] = outs[0]
    vx_ref[...] = outs[2] * outs[1]


def hy_in(h, mod, is_ctx, w_in, b_in, conv_w, conv_b):
    bsz, L, d = h.shape
    tm = min(HY_ROWS, L)
    per = tm // HY_HALO
    n_halo = L // HY_HALO
    row = (lambda b, i: (bsz, 0, 0)) if is_ctx else (lambda b, i: (b, 0, 0))
    tile = pl.BlockSpec((None, tm, d), lambda b, i: (b, i, 0))
    shp = jax.ShapeDtypeStruct((bsz, L, d), F32)
    const = lambda n: (lambda b, i: (0,) * n)
    return pl.pallas_call(
        functools.partial(_hy_in_kernel, tm=tm),
        out_shape=(shp, shp),
        grid=(bsz, L // tm),
        in_specs=[tile,
                  pl.BlockSpec((None, HY_HALO, d), lambda b, i: (b, jnp.maximum(i * per - 1, 0), 0)),
                  pl.BlockSpec((None, HY_HALO, d), lambda b, i: (b, jnp.minimum((i + 1) * per, n_halo - 1), 0)),
                  pl.BlockSpec((None, N_MOD, d), row),
                  pl.BlockSpec((3, d, d), const(3), pipeline_mode=pl.Buffered(1)),
                  pl.BlockSpec((3, 1, d), const(3)),
                  pl.BlockSpec((3, 3, 1, d), const(4)),
                  pl.BlockSpec((3, 1, d), const(3))],
        out_specs=(tile, tile),
        compiler_params=_params(2),
        name="hy_in",
    )(h, h, h, mod, w_in.astype(BF16).reshape(d, 3, d).transpose(1, 0, 2), b_in.reshape(3, 1, d),
      conv_w.reshape(3, 3, 1, d), conv_b.reshape(3, 1, d))


def _hy_filter_kernel(band_ref, w1_ref, b1_ref, w2_ref, b2_ref, w3_ref, b3_ref, w4_ref, fr_ref, dl_ref, o_ref,
                      *, tr, L, d):
    n = pl.program_id(0) * tr + lax.broadcasted_iota(jnp.int32, (tr, 1), 0)
    j = jnp.where(n < L, n, 2 * L - n).astype(F32)
    t = j / (L - 1)
    lane = lax.broadcasted_iota(jnp.int32, (1, 128), 1)
    ang = (2 * math.pi * j / L) * band_ref[...]
    feats = jnp.where(lane == 0, t,
                      jnp.where(lane <= HY_BANDS, jnp.cos(ang),
                                jnp.where(lane <= 2 * HY_BANDS, -jnp.sin(ang), 0.0)))
    fr = fr_ref[...]
    a = jnp.sin(fr * (_dot_f32(feats, w1_ref[...]) + b1_ref[...]))
    a = jnp.sin(fr * (_dot_f32(a, w2_ref[...]) + b2_ref[...]))
    a = jnp.sin(fr * (_dot_f32(a, w3_ref[...]) + b3_ref[...]))
    kv = _dot_f32(a, w4_ref[...])
    val = jnp.where(n < L, kv[:, :d], kv[:, d:]) * jnp.exp(-t * dl_ref[...])
    o_ref[...] = jnp.where(n == L, 0.0, val)


def hy_filter(L, w1, b1, w2, b2, w3, b3, w4, sin_freq):
    d = w4.shape[1] // 2
    tr = 256
    band = np.zeros((1, 128), np.float32)
    bands = np.linspace(1e-4, HY_BANDS - 1, HY_BANDS, dtype=np.float32)
    band[0, 1:1 + HY_BANDS] = bands
    band[0, 1 + HY_BANDS:1 + 2 * HY_BANDS] = bands
    w1p = jnp.zeros((128, HY_ORDER), F32).at[:HY_EMB].set(w1)
    deltas = np.abs(np.linspace(math.log(HY_TARGET) / HY_SLOW_PCT, math.log(HY_TARGET) / HY_FAST_PCT, d,
                                dtype=np.float32)).reshape(1, d)
    full = lambda a: pl.BlockSpec(a.shape, lambda i: (0,) * a.ndim)
    args = [jnp.asarray(band), w1p, b1.reshape(1, -1), w2, b2.reshape(1, -1), w3, b3.reshape(1, -1), w4,
            sin_freq.reshape(1, -1), jnp.asarray(deltas)]
    return pl.pallas_call(
        functools.partial(_hy_filter_kernel, tr=tr, L=L, d=d),
        out_shape=jax.ShapeDtypeStruct((2 * L, d), F32),
        grid=(2 * L // tr,),
        in_specs=[full(a) for a in args],
        out_specs=pl.BlockSpec((tr, d), lambda i: (i, 0)),
        compiler_params=_params(1),
        name="hy_filter",
    )(*args)


def _cis(num, den, sign):
    ph = (num % den).astype(F32) * (2 * math.pi / den)
    return jnp.cos(ph), sign * jnp.sin(ph)


def _cblock(re, im):
    return jnp.concatenate([jnp.concatenate([re, -im], axis=-1), jnp.concatenate([im, re], axis=-1)], axis=-2)


def _left_kernel(m_ref, x_ref, o_ref):
    o_ref[...] = jnp.dot(m_ref[...], x_ref[...].astype(BF16), preferred_element_type=F32)


def fft_left(mat, x):
    P, K, lanes = x.shape
    M = mat.shape[0]
    nb = min(FFT_LANES, lanes)
    return pl.pallas_call(
        _left_kernel,
        out_shape=jax.ShapeDtypeStruct((P, M, lanes), F32),
        grid=(P, lanes // nb),
        in_specs=[pl.BlockSpec((M, K), lambda p, j: (0, 0)),
                  pl.BlockSpec((None, K, nb), lambda p, j: (p, 0, j))],
        out_specs=pl.BlockSpec((None, M, nb), lambda p, j: (p, 0, j)),
        compiler_params=_params(2),
        name="fft_left",
    )(mat.astype(BF16), x)


def _mid_kernel(*refs, pairs, n2, with_filter):
    if with_filter:
        a_ref, k_ref, mf_ref, mi_ref, o_ref = refs
    else:
        a_ref, mf_ref, o_ref = refs
    for p in range(pairs):
        a = jnp.concatenate([a_ref[p, 0], a_ref[p, 1]], axis=0).astype(BF16)
        x = jnp.dot(mf_ref[...], a, preferred_element_type=F32)
        xr, xi = x[:n2], x[n2:]
        if with_filter:
            kr, ki = k_ref[0], k_ref[1]
            y = jnp.concatenate([xr * kr - xi * ki, xr * ki + xi * kr], axis=0).astype(BF16)
            x = jnp.dot(mi_ref[...], y, preferred_element_type=F32)
            xr, xi = x[:n2], x[n2:]
        o_ref[p, 0] = xr
        o_ref[p, 1] = xi


def fft_mid(a, mf, kspec=None, mi=None):
    P, _, n1, n2, d = a.shape
    cb = min(FFT_CH, d)
    a_spec = pl.BlockSpec((P, 2, None, n2, cb), lambda c, f: (0, 0, f, 0, c))
    m_spec = pl.BlockSpec((None, 2 * n2, 2 * n2), lambda c, f: (f, 0, 0))
    if kspec is None:
        args, specs = [a, mf], [a_spec, m_spec]
    else:
        args = [a, kspec, mf, mi]
        specs = [a_spec, pl.BlockSpec((2, None, n2, cb), lambda c, f: (0, f, 0, c)), m_spec, m_spec]
    return pl.pallas_call(
        functools.partial(_mid_kernel, pairs=P, n2=n2, with_filter=kspec is not None),
        out_shape=jax.ShapeDtypeStruct(a.shape, F32),
        grid=(d // cb, n1),
        in_specs=specs,
        out_specs=a_spec,
        compiler_params=_params(2),
        name="fft_mid",
    )(*args)


def long_conv_two_stage(vx, kc):
    bsz, L, d = vx.shape
    n = 2 * L
    n2 = FFT_N2
    n1 = n // n2
    P = bsz // 2
    i1 = jnp.arange(n1, dtype=jnp.int32)
    i2 = jnp.arange(n2, dtype=jnp.int32)
    fr, fi = _cis(i1[:, None] * i1[None, :], n1, -1.0)
    ph = i2[None, None, :] * (i1[:, None, None] + n1 * i2[None, :, None])
    er, ei = _cis(ph, n, -1.0)
    mf = _cblock(er, ei).astype(BF16)
    mi = _cblock(er.transpose(0, 2, 1), -ei.transpose(0, 2, 1)).astype(BF16)
    gr, gi = _cis(i1[: n1 // 2, None] * i1[None, :], n1, 1.0)
    g = _cblock(gr, gi) / n
    ak = fft_left(jnp.concatenate([fr, fi], axis=0), kc.reshape(1, n1, n2 * d))
    kspec = fft_mid(ak.reshape(1, 2, n1, n2, d), mf)[0]
    a = fft_left(_cblock(fr[:, : n1 // 2], fi[:, : n1 // 2]), vx.reshape(P, n1, n2 * d))
    bmid = fft_mid(a.reshape(P, 2, n1, n2, d), mf, kspec, mi)
    y = fft_left(g, bmid.reshape(P, 2 * n1, n2 * d))
    return y.reshape(bsz, L, d)


def _direct_conv_kernel(x_ref, k_ref, f_ref, fk_ref, g_ref, o_ref, *, n):
    kf = jnp.dot(fk_ref[...], k_ref[...].astype(BF16), preferred_element_type=F32)
    x = jnp.dot(f_ref[...], x_ref[...].astype(BF16), preferred_element_type=F32)
    xr, xi, kr, ki = x[:n], x[n:], kf[:n], kf[n:]
    y = jnp.concatenate([xr * kr - xi * ki, xr * ki + xi * kr], axis=0).astype(BF16)
    o_ref[...] = jnp.dot(g_ref[...], y, preferred_element_type=F32)


def long_conv_direct(vx, kc):
    bsz, L, d = vx.shape
    n = 2 * L
    P = bsz // 2
    idx = jnp.arange(n, dtype=jnp.int32)
    fr, fi = _cis(idx[:, None] * idx[None, :], n, -1.0)
    f_data = _cblock(fr[:, :L], fi[:, :L]).astype(BF16)
    f_filt = jnp.concatenate([fr, fi], axis=0).astype(BF16)
    g = (_cblock(fr[:L, :], -fi[:L, :]) / n).astype(BF16)
    full = lambda a: pl.BlockSpec(a.shape, lambda p: (0,) * a.ndim)
    y = pl.pallas_call(
        functools.partial(_direct_conv_kernel, n=n),
        out_shape=jax.ShapeDtypeStruct((P, n, d), F32),
        grid=(P,),
        in_specs=[pl.BlockSpec((None, n, d), lambda p: (p, 0, 0)), full(kc), full(f_data), full(f_filt), full(g)],
        out_specs=pl.BlockSpec((None, n, d), lambda p: (p, 0, 0)),
        compiler_params=_params(1),
        name="conv_direct",
    )(vx.reshape(P, n, d), kc, f_data, f_filt, g)
    return y.reshape(bsz, L, d)


def _hy_tail_kernel(h_ref, x0_ref, vx_ref, cv_ref, mod_ref, fb_ref, w_ref, b_ref, o_ref):
    y = (x0_ref[...] * (cv_ref[...] + vx_ref[...] * fb_ref[...])).astype(BF16)
    o_ref[...] = h_ref[...] + mod_ref[5:6, :] * (jnp.dot(y, w_ref[...], preferred_element_type=F32) + b_ref[...])


def hy_tail(h, x0, vx, conv, mod, is_ctx, filt_bias, w_out, b_out):
    bsz, L, d = h.shape
    tm = min(HY_ROWS, L)
    row = (lambda b, i: (bsz, 0, 0)) if is_ctx else (lambda b, i: (b, 0, 0))
    tile = pl.BlockSpec((None, tm, d), lambda b, i: (b, i, 0))
    vec = pl.BlockSpec((1, d), lambda b, i: (0, 0))
    return pl.pallas_call(
        _hy_tail_kernel,
        out_shape=jax.ShapeDtypeStruct((bsz, L, d), F32),
        grid=(bsz, L // tm),
        in_specs=[tile, tile, tile, tile, pl.BlockSpec((None, N_MOD, d), row), vec,
                  pl.BlockSpec((d, d), lambda b, i: (0, 0)), vec],
        out_specs=tile,
        compiler_params=_params(2),
        name="hy_tail",
    )(h, x0, vx, conv, mod, filt_bias.reshape(1, d), w_out.astype(BF16), b_out.reshape(1, d))


def hyena_layer(h, mod, is_ctx, w_in, b_in, conv_w, conv_b, w1, b1, w2, b2, w3, b3, w4, sin_freq, filt_bias,
                w_out, b_out):
    L = h.shape[1]
    kc = hy_filter(L, w1, b1, w2, b2, w3, b3, w4, sin_freq)
    x0, vx = hy_in(h, mod, is_ctx, w_in, b_in, conv_w, conv_b)
    conv = long_conv_direct(vx, kc) if L <= FFT_DIRECT_MAX else long_conv_two_stage(vx, kc)
    return hy_tail(h, x0, vx, conv, mod, is_ctx, filt_bias, w_out, b_out)


def kernel(x, c, ctx, c_ctx, w_ada, b_ada, ffn_w_gate, ffn_w_up, ffn_w_down, hg_w_in, hg_lb_logits, hg_norm_gain,
           hg_w_out, pool_w, pool_scale, hy_w_in, hy_b_in, hy_conv_w, hy_conv_b, hy_w1, hy_b1, hy_w2, hy_b2, hy_w3,
           hy_b3, hy_w4, hy_sin_freq, hy_filt_bias, hy_w_out, hy_b_out, final_gain):
    depth = w_ada.shape[0]
    mods = ada_table(c, c_ctx, w_ada, b_ada)
    h, hc = x, ctx
    for i in range(depth):
        kind, j = i % 3, i // 3
        last = i == depth - 1
        ctx_live = (not last) or kind == 0
        mod = mods[i]
        ffn = lambda a, is_ctx, k0, s, gain=None: ffn_half_step(
            a, mod, is_ctx, k0, ffn_w_gate[i, s], ffn_w_up[i, s], ffn_w_down[i, s], gain)
        h = ffn(h, False, 0, 0)
        if ctx_live:
            hc = ffn(hc, True, 0, 0)
        if kind == 0:
            h, hc = hgrn2_layer(h, hc, mod, hg_w_in[j], hg_lb_logits, j, hg_norm_gain[j], hg_w_out[j], not last)
        elif kind == 1:
            h = pool_layer(h, mod, False, pool_w[j], pool_scale[j])
            if not last:
                hc = pool_layer(hc, mod, True, pool_w[j], pool_scale[j])
        else:
            hy = (hy_w_in[j], hy_b_in[j], hy_conv_w[j], hy_conv_b[j], hy_w1[j], hy_b1[j], hy_w2[j], hy_b2[j],
                  hy_w3[j], hy_b3[j], hy_w4[j], hy_sin_freq[j], hy_filt_bias[j], hy_w_out[j], hy_b_out[j])
            h = hyena_layer(h, mod, False, *hy)
            if not last:
                hc = hyena_layer(hc, mod, True, *hy)
        h = ffn(h, False, 6, 1, final_gain if last else None)
        if not last:
            hc = ffn(hc, True, 6, 1)
    return h
```

```python
import functools
import math

import jax
import jax.numpy as jnp
import numpy as np
from jax import lax
from jax.experimental import pallas as pl
from jax.experimental.pallas import tpu as pltpu

F32 = jnp.float32
BF16 = jnp.bfloat16

D_MODEL = 1024
DEPTH = 4
N_MOD = 9
D_FF = 2816
EPS = 1e-6
GRID_W = 64

HG_HEADS = 8
HG_DK = 128

POOL_WINDOWS = (2, 4, 8, 16)
POOL_GC = D_MODEL // len(POOL_WINDOWS)

HY_EMB = 33
HY_BANDS = (HY_EMB - 1) // 2
HY_ORDER = 64
HY_TARGET = 1e-2
HY_FAST_PCT = 0.3
HY_SLOW_PCT = 1.5

MOD_ROWS = 8
V7X_VMEM_LIMIT = 56 * 1024 * 1024


def _params(n_grid, vmem=None):
    return pltpu.CompilerParams(dimension_semantics=("arbitrary",) * n_grid,
                                vmem_limit_bytes=vmem or V7X_VMEM_LIMIT)


def _silu(x):
    return x * jax.nn.sigmoid(x)


def _rms(x):
    return x * lax.rsqrt(jnp.mean(x * x, axis=-1, keepdims=True) + EPS)


def _modulate(x, shift, scale):
    return _rms(x) * (1.0 + scale) + shift


def _ada_kernel(cc_ref, w_ref, b_ref, o_ref):
    s = _silu(cc_ref[...]).astype(BF16)
    o_ref[...] = jnp.dot(s, w_ref[...].astype(BF16), preferred_element_type=F32) + b_ref[...]


def ada_table(c, c_ctx, w_ada, b_ada):
    bsz, d = c.shape
    depth, _, nd = w_ada.shape
    cc = jnp.zeros((MOD_ROWS, d), F32).at[:bsz].set(c).at[bsz].set(c_ctx)
    tn = 1024
    out = pl.pallas_call(
        _ada_kernel,
        out_shape=jax.ShapeDtypeStruct((depth, MOD_ROWS, nd), F32),
        grid=(depth, nd // tn),
        in_specs=[pl.BlockSpec((MOD_ROWS, d), lambda l, j: (0, 0)),
                  pl.BlockSpec((None, d, tn), lambda l, j: (l, 0, j)),
                  pl.BlockSpec((None, 1, tn), lambda l, j: (l, 0, j))],
        out_specs=pl.BlockSpec((None, MOD_ROWS, tn), lambda l, j: (l, 0, j)),
        compiler_params=_params(2),
        name="ada_table",
    )(cc, w_ada, b_ada.reshape(depth, 1, nd))
    return out.reshape(depth, MOD_ROWS, N_MOD, d)


FFN_CHUNK = 256
FFN_ROWS = 512


def _ffn_kernel(h_ref, mod_ref, wg_ref, wu_ref, wd_ref, *rest, k0, n_chunks, final):
    if final:
        gain_ref, o_ref, acc_ref = rest
    else:
        o_ref, acc_ref = rest
    x = h_ref[...]
    y = _modulate(x, mod_ref[k0:k0 + 1, :], mod_ref[k0 + 1:k0 + 2, :]).astype(BF16)
    acc_ref[...] = jnp.zeros_like(acc_ref)

    def body(j, carry):
        g = jnp.dot(y, wg_ref[j], preferred_element_type=F32)
        u = jnp.dot(y, wu_ref[j], preferred_element_type=F32)
        a = (_silu(g) * u).astype(BF16)
        acc_ref[...] += jnp.dot(a, wd_ref[j], preferred_element_type=F32)
        return carry

    lax.fori_loop(0, n_chunks, body, 0)
    out = x + (0.5 * mod_ref[k0 + 2:k0 + 3, :]) * acc_ref[...]
    if final:
        out = _rms(out) * gain_ref[...]
    o_ref[...] = out


def _chunk_cols(w):
    d, f = w.shape
    return w.astype(BF16).reshape(d, f // FFN_CHUNK, FFN_CHUNK).transpose(1, 0, 2)


def ffn_half_step(h, mod, is_ctx, k0, w_gate, w_up, w_down, final_gain=None):
    bsz, L, d = h.shape
    f = w_gate.shape[1]
    n_chunks = f // FFN_CHUNK
    tm = min(FFN_ROWS, L)
    wg = _chunk_cols(w_gate)
    wu = _chunk_cols(w_up)
    wd = w_down.astype(BF16).reshape(n_chunks, FFN_CHUNK, d)
    row = (lambda b, i: (bsz, 0, 0)) if is_ctx else (lambda b, i: (b, 0, 0))
    const3 = lambda b, i: (0, 0, 0)
    single = pl.Buffered(1)
    in_specs = [pl.BlockSpec((None, tm, d), lambda b, i: (b, i, 0)),
                pl.BlockSpec((None, N_MOD, d), row),
                pl.BlockSpec((n_chunks, d, FFN_CHUNK), const3, pipeline_mode=single),
                pl.BlockSpec((n_chunks, d, FFN_CHUNK), const3, pipeline_mode=single),
                pl.BlockSpec((n_chunks, FFN_CHUNK, d), const3, pipeline_mode=single)]
    args = [h, mod, wg, wu, wd]
    if final_gain is not None:
        in_specs.append(pl.BlockSpec((1, d), lambda b, i: (0, 0)))
        args.append(final_gain.reshape(1, d))
    return pl.pallas_call(
        functools.partial(_ffn_kernel, k0=k0, n_chunks=n_chunks, final=final_gain is not None),
        out_shape=jax.ShapeDtypeStruct((bsz, L, d), F32),
        grid=(bsz, L // tm),
        in_specs=in_specs,
        out_specs=pl.BlockSpec((None, tm, d), lambda b, i: (b, i, 0)),
        scratch_shapes=[pltpu.VMEM((tm, d), F32)],
        compiler_params=_params(2),
        name="ffn_half_step",
    )(*args)


HG_ROWS = 256
GLA_TILE = 256
GLA_BLOCK = 32
GLA_SAFE_LOG_DECAY = 80.0


def _split3(x):
    x1 = x.astype(BF16)
    r1 = x - x1.astype(F32)
    x2 = r1.astype(BF16)
    x3 = (r1 - x2.astype(F32)).astype(BF16)
    return x1, x2, x3


def _dot_exact_lhs(a_bf16, x):
    x1, x2, x3 = _split3(x)
    return (jnp.dot(a_bf16, x1, preferred_element_type=F32)
            + jnp.dot(a_bf16, x2, preferred_element_type=F32)
            + jnp.dot(a_bf16, x3, preferred_element_type=F32))


def _hg_proj_kernel(h_ref, mod_ref, w_ref, lbl_ref, q_ref, kf_ref, gf_ref, kb_ref, gb_ref, v_ref, sg_ref, *, layer_j):
    x = h_ref[...]
    u = _modulate(x, mod_ref[3:4, :], mod_ref[4:5, :]).astype(BF16)
    q_ref[...] = _silu(jnp.dot(u, w_ref[0], preferred_element_type=F32))
    v_ref[...] = jnp.dot(u, w_ref[3], preferred_element_type=F32)
    sg_ref[...] = _silu(jnp.dot(u, w_ref[4], preferred_element_type=F32))
    for d, (k_ref, g_ref) in enumerate(((kf_ref, gf_ref), (kb_ref, gb_ref))):
        logits = lbl_ref[d]
        e = jnp.exp(logits - jnp.max(logits, axis=0, keepdims=True))
        p = e / jnp.sum(e, axis=0, keepdims=True)
        lb = jnp.sum(p[:layer_j + 1], axis=0, keepdims=True) - p[0:1]
        z = jnp.dot(u, w_ref[1 + d], preferred_element_type=F32)
        g_ref[...] = jnp.log(lb + (1.0 - lb) * jax.nn.sigmoid(z))
        k_ref[...] = (1.0 - lb) * jax.nn.sigmoid(-z)


def hg_project(h, mod, is_ctx, w_in, lb_logits, layer_j):
    bsz, L, d = h.shape
    tm = min(HG_ROWS, L)
    n_blk = w_in.shape[1] // d
    w = w_in.astype(BF16).reshape(d, n_blk, d).transpose(1, 0, 2)
    row = (lambda b, i: (bsz, 0, 0)) if is_ctx else (lambda b, i: (b, 0, 0))
    tile = pl.BlockSpec((None, tm, d), lambda b, i: (b, i, 0))
    shp = jax.ShapeDtypeStruct((bsz, L, d), F32)
    return pl.pallas_call(
        functools.partial(_hg_proj_kernel, layer_j=layer_j),
        out_shape=(shp,) * 7,
        grid=(bsz, L // tm),
        in_specs=[tile,
                  pl.BlockSpec((None, N_MOD, d), row),
                  pl.BlockSpec((n_blk, d, d), lambda b, i: (0, 0, 0), pipeline_mode=pl.Buffered(1)),
                  pl.BlockSpec(lb_logits.shape, lambda b, i: (0, 0, 0))],
        out_specs=(tile,) * 7,
        compiler_params=_params(2),
        name="hg_project",
    )(h, mod, w, lb_logits)


def _gla_consts(tl, blk):
    r = np.arange(tl)
    same = (r[:, None] // blk) == (r[None, :] // blk)
    tri_f = same & (r[None, :] <= r[:, None])
    tri_b = same & (r[None, :] >= r[:, None])
    return (jnp.asarray(np.stack([tri_f, tri_b]), BF16), jnp.asarray(same, BF16))


def _gla_kernel(qf_ref, qb_ref, kf_ref, gf_ref, kb_ref, gb_ref, vf_ref, vb_ref, tri_ref, ones_ref, s0_ref,
                of_ref, ob_ref, sfin_ref,
                st_ref, bc_ref, qd_ref, kt_ref, ke_ref, vv_ref, dec_ref, *, n_blk, blk, heads, dk):
    i = pl.program_id(1)

    @pl.when(i == 0)
    def _():
        st_ref[...] = s0_ref[...]

    dirs = ((qf_ref, kf_ref, gf_ref, vf_ref, of_ref), (qb_ref, kb_ref, gb_ref, vb_ref, ob_ref))
    worst = None
    for d, (q_ref, k_ref, g_ref, v_ref, _) in enumerate(dirs):
        g = g_ref[...]
        bc = _dot_exact_lhs(tri_ref[d], g)
        bl = _dot_exact_lhs(ones_ref[...], g)
        bc_ref[d] = bc
        dec_ref[d] = jnp.exp(bl)
        qd_ref[d] = (q_ref[...] * jnp.exp(bc)).astype(BF16)
        ke_ref[d] = (k_ref[...] * jnp.exp(bl - bc)).astype(BF16)
        vv_ref[d] = v_ref[...].astype(BF16)
        m = jnp.min(bl)
        worst = m if worst is None else jnp.minimum(worst, m)
    safe = worst >= -GLA_SAFE_LOG_DECAY

    @pl.when(safe)
    def _():
        for d, (_, k_ref, _, _, _) in enumerate(dirs):
            kt_ref[d] = (k_ref[...] * jnp.exp(-bc_ref[d])).astype(BF16)

    t_io = lax.broadcasted_iota(jnp.int32, (blk, blk), 0)
    s_io = lax.broadcasted_iota(jnp.int32, (blk, blk), 1)
    t_io_w = lax.broadcasted_iota(jnp.int32, (blk, dk), 0)
    sub_io = lax.broadcasted_iota(jnp.int32, (8, dk), 0)
    nt =(((1,), (1,)), ((), ()))
    tn = (((0,), (0,)), ((), ()))

    def run(fast):
        def step(jb, carry):
            chains = []
            for d in range(2):
                r0 = pl.multiple_of((jb if d == 0 else n_blk - 1 - jb) * blk, blk)
                for hh in range(heads):
                    chains.append((d, hh, r0, pl.ds(r0, blk), slice(hh * dk, (hh + 1) * dk)))
            scores, carried = [], []
            for d, hh, r0, rows, cols in chains:
                q_ref, k_ref = dirs[d][0], dirs[d][1]
                qd = qd_ref[d, rows, cols]
                if fast:
                    sc = lax.dot_general(qd, kt_ref[d, rows, cols], nt, preferred_element_type=F32)
                    sc = jnp.where((s_io <= t_io) if d == 0 else (s_io >= t_io), sc, 0.0)
                else:
                    qq = q_ref[rows, cols]
                    bb = bc_ref[d, rows, cols]

                    def col(s, sc, d=d, r0=r0, cols=cols, qq=qq, bb=bb, k_ref=k_ref):
                        grp = pl.ds(pl.multiple_of(r0 + (s // 8) * 8, 8), 8)
                        pick = sub_io == s % 8
                        brow = jnp.sum(jnp.where(pick, bc_ref[d, grp, cols], 0.0), axis=0, keepdims=True)
                        krow = jnp.sum(jnp.where(pick, k_ref[grp, cols], 0.0), axis=0, keepdims=True)
                        ok = (t_io_w >= s) if d == 0 else (t_io_w <= s)
                        p = jnp.where(ok, qq * krow * jnp.exp(jnp.minimum(bb - brow, 0.0)), 0.0)
                        return jnp.where(s_io == s, jnp.sum(p, axis=-1, keepdims=True), sc)

                    sc = lax.fori_loop(0, blk, col, jnp.zeros((blk, blk), F32))
                scores.append(sc.astype(BF16))
                carried.append(lax.dot_general(qd, st_ref[d, hh].astype(BF16), nt, preferred_element_type=F32))
            for (d, hh, r0, rows, cols), sc, from_state in zip(chains, scores, carried):
                o_ref = dirs[d][4]
                o_ref[rows, cols] = jnp.dot(sc, vv_ref[d, rows, cols], preferred_element_type=F32) + from_state
            for d, hh, r0, rows, cols in chains:
                dec = dec_ref[d, pl.ds(r0, 1), cols]
                st_ref[d, hh] = st_ref[d, hh] * dec + lax.dot_general(
                    vv_ref[d, rows, cols], ke_ref[d, rows, cols], tn, preferred_element_type=F32)
            return carry

        lax.fori_loop(0, n_blk, step, 0)

    @pl.when(safe)
    def _():
        run(True)

    @pl.when(jnp.logical_not(safe))
    def _():
        run(False)

    @pl.when(i == pl.num_programs(1) - 1)
    def _():
        sfin_ref[...] = st_ref[...]


def gla_bidirectional(q, k_f, g_f, k_b, g_b, v, s0):
    bsz, L, d = q.shape
    tl = min(GLA_TILE, L)
    nt = L // tl
    n_blk = tl // GLA_BLOCK
    tri, ones = _gla_consts(tl, GLA_BLOCK)
    fwd = pl.BlockSpec((None, tl, d), lambda b, i: (b, i, 0))
    bwd = pl.BlockSpec((None, tl, d), lambda b, i: (b, nt - 1 - i, 0))
    st_spec = pl.BlockSpec((2, None, HG_HEADS, HG_DK, HG_DK), lambda b, i: (0, b, 0, 0, 0))
    shp = jax.ShapeDtypeStruct((bsz, L, d), F32)
    return pl.pallas_call(
        functools.partial(_gla_kernel, n_blk=n_blk, blk=GLA_BLOCK, heads=HG_HEADS, dk=HG_DK),
        out_shape=(shp, shp, jax.ShapeDtypeStruct(s0.shape, F32)),
        grid=(bsz, nt),
        in_specs=[fwd, bwd, fwd, fwd, bwd, bwd, fwd, bwd,
                  pl.BlockSpec((2, tl, tl), lambda b, i: (0, 0, 0)),
                  pl.BlockSpec((tl, tl), lambda b, i: (0, 0)),
                  st_spec],
        out_specs=(fwd, bwd, st_spec),
        scratch_shapes=[pltpu.VMEM((2, HG_HEADS, HG_DK, HG_DK), F32),
                        pltpu.VMEM((2, tl, d), F32),
                        pltpu.VMEM((2, tl, d), BF16),
                        pltpu.VMEM((2, tl, d), BF16),
                        pltpu.VMEM((2, tl, d), BF16),
                        pltpu.VMEM((2, tl, d), BF16),
                        pltpu.VMEM((2, tl, d), F32)],
        compiler_params=_params(2),
        name="gla_bidirectional",
    )(q, q, k_f, g_f, k_b, g_b, v, v, tri, ones, s0)


def _hg_readout_kernel(h_ref, of_ref, ob_ref, sg_ref, mod_ref, gain_ref, w_ref, o_ref, *, heads, dk):
    o = of_ref[...] + ob_ref[...]
    parts = []
    for hh in range(heads):
        parts.append(_rms(o[:, hh * dk:(hh + 1) * dk]))
    on = jnp.concatenate(parts, axis=-1)
    y = (on * gain_ref[...] * sg_ref[...]).astype(BF16)
    o_ref[...] = h_ref[...] + mod_ref[5:6, :] * jnp.dot(y, w_ref[...], preferred_element_type=F32)


def hg_readout(h, o_f, o_b, sg, mod, is_ctx, norm_gain, w_out):
    bsz, L, d = h.shape
    tm = min(HG_ROWS, L)
    row = (lambda b, i: (bsz, 0, 0)) if is_ctx else (lambda b, i: (b, 0, 0))
    tile = pl.BlockSpec((None, tm, d), lambda b, i: (b, i, 0))
    return pl.pallas_call(
        functools.partial(_hg_readout_kernel, heads=HG_HEADS, dk=HG_DK),
        out_shape=jax.ShapeDtypeStruct((bsz, L, d), F32),
        grid=(bsz, L // tm),
        in_specs=[tile, tile, tile, tile,
                  pl.BlockSpec((None, N_MOD, d), row),
                  pl.BlockSpec((1, d), lambda b, i: (0, 0)),
                  pl.BlockSpec((d, d), lambda b, i: (0, 0))],
        out_specs=tile,
        compiler_params=_params(2),
        name="hg_readout",
    )(h, o_f, o_b, sg, mod, norm_gain.reshape(1, d), w_out.astype(BF16))


def hgrn2_layer(h, hc, mod, w_in, lb_logits, layer_j, norm_gain, w_out, ctx_out):
    bsz = h.shape[0]
    qc, kfc, gfc, kbc, gbc, vc, sgc = hg_project(hc, mod, True, w_in, lb_logits, layer_j)
    ql, kfl, gfl, kbl, gbl, vl, sgl = hg_project(h, mod, False, w_in, lb_logits, layer_j)
    s0 = jnp.zeros((2, bsz, HG_HEADS, HG_DK, HG_DK), F32)
    ocf, ocb, s_ctx = gla_bidirectional(qc, kfc, gfc, kbc, gbc, vc, s0)
    olf, olb, _ = gla_bidirectional(ql, kfl, gfl, kbl, gbl, vl, s_ctx)
    h = hg_readout(h, olf, olb, sgl, mod, False, norm_gain, w_out)
    if ctx_out:
        hc = hg_readout(hc, ocf, ocb, sgc, mod, True, norm_gain, w_out)
    return h, hc


POOL_ROWS = 256


def _window_matrix(n, w, period=None):
    p = np.arange(n)
    period = period or n
    lo, hi = p - w // 2, p + (w - w // 2)
    same = (p[:, None] // period) == (p[None, :] // period)
    return same & (p[None, :] >= lo[:, None]) & (p[None, :] < hi[:, None])


def _window_count(pos, n, w):
    return (jnp.clip(pos + (w - w // 2), 0, n) - jnp.clip(pos - w // 2, 0, n)).astype(F32)


def _pool_row_kernel(h_ref, mod_ref, a_ref, o_ref, *, n_rows):
    u = _modulate(h_ref[...], mod_ref[3:4, :], mod_ref[4:5, :])
    r = lax.broadcasted_iota(jnp.int32, (n_rows, 1), 0)
    for gi, w in enumerate(POOL_WINDOWS):
        cols = slice(gi * POOL_GC, (gi + 1) * POOL_GC)
        o_ref[:, cols] = _dot_exact_lhs(a_ref[gi], u[:, cols]) / _window_count(r, n_rows, w)


def _pool_main_kernel(*refs, period, on_grid):
    if on_grid:
        h_ref, r_ref, mod_ref, a_ref, wp_ref, sc_ref, o_ref = refs
    else:
        h_ref, mod_ref, a_ref, wp_ref, sc_ref, o_ref = refs
    x = h_ref[...]
    u = _modulate(x, mod_ref[3:4, :], mod_ref[4:5, :])
    src = r_ref[...] if on_grid else u
    pos = lax.broadcasted_iota(jnp.int32, (x.shape[0], 1), 0) % period
    for gi, w in enumerate(POOL_WINDOWS):
        cols = slice(gi * POOL_GC, (gi + 1) * POOL_GC)
        mean = _dot_exact_lhs(a_ref[gi], src[:, cols]) / _window_count(pos, period, w)
        pooled = (mean - u[:, cols]).astype(BF16)
        y = jnp.dot(pooled, wp_ref[gi], preferred_element_type=F32) * sc_ref[:, cols]
        o_ref[:, cols] = x[:, cols] + mod_ref[5:6, cols] * y


def pool_layer(h, mod, is_ctx, w_pool, scale):
    bsz, L, d = h.shape
    wp = w_pool.astype(BF16)
    row = (lambda b, i: (bsz, 0, 0)) if is_ctx else (lambda b, i: (b, 0, 0))
    mod_spec = pl.BlockSpec((None, N_MOD, d), row)
    tm = min(POOL_ROWS, L)
    tile = pl.BlockSpec((None, tm, d), lambda b, i: (b, i, 0))
    ng = len(POOL_WINDOWS)
    args, specs = [h], [tile]
    if is_ctx:
        assert tm == L, "the 1D windows need the whole context in one tile"
        period = L
    else:
        n_rows = L // GRID_W
        a_row = jnp.asarray(np.stack([_window_matrix(n_rows, w) for w in POOL_WINDOWS]), BF16)
        rows = pl.pallas_call(
            functools.partial(_pool_row_kernel, n_rows=n_rows),
            out_shape=jax.ShapeDtypeStruct((bsz, n_rows, GRID_W * d), F32),
            grid=(bsz, GRID_W),
            in_specs=[pl.BlockSpec((None, n_rows, d), lambda b, c: (b, 0, c)),
                      pl.BlockSpec((None, N_MOD, d), lambda b, c: (b, 0, 0)),
                      pl.BlockSpec((ng, n_rows, n_rows), lambda b, c: (0, 0, 0))],
            out_specs=pl.BlockSpec((None, n_rows, d), lambda b, c: (b, 0, c)),
            compiler_params=_params(2),
            name="pool_rows",
        )(h.reshape(bsz, n_rows, GRID_W * d), mod, a_row)
        args.append(rows.reshape(bsz, L, d))
        specs.append(tile)
        period = GRID_W
    a_col = jnp.asarray(np.stack([_window_matrix(tm, w, period) for w in POOL_WINDOWS]), BF16)
    args += [mod, a_col, wp, scale.reshape(1, d)]
    specs += [mod_spec,
              pl.BlockSpec((ng, tm, tm), lambda b, i: (0, 0, 0)),
              pl.BlockSpec((ng, POOL_GC, POOL_GC), lambda b, i: (0, 0, 0)),
              pl.BlockSpec((1, d), lambda b, i: (0, 0))]
    return pl.pallas_call(
        functools.partial(_pool_main_kernel, period=period, on_grid=not is_ctx),
        out_shape=jax.ShapeDtypeStruct((bsz, L, d), F32),
        grid=(bsz, L // tm),
        in_specs=specs,
        out_specs=tile,
        compiler_params=_params(2),
        name="pool_main",
    )(*args)


HY_ROWS = 256
HY_HALO = 8
FFT_N2 = 128
FFT_LANES = 2048
FFT_CH = 512
FFT_DIRECT_MAX = 512


def _dot_f32(a, b):
    a1, a2, a3 = _split3(a)
    b1, b2, b3 = _split3(b)
    d = lambda x, y: jnp.dot(x, y, preferred_element_type=F32)
    return d(a1, b1) + (d(a1, b2) + d(a2, b1)) + (d(a1, b3) + d(a2, b2) + d(a3, b1))


def _hy_in_kernel(h_ref, hp_ref, hn_ref, mod_ref, w_ref, b_ref, cw_ref, cb_ref, x0_ref, vx_ref, *, tm):
    i = pl.program_id(1)
    last = pl.num_programs(1) - 1
    sh, sc = mod_ref[3:4, :], mod_ref[4:5, :]
    u = _modulate(h_ref[...], sh, sc).astype(BF16)
    up = _modulate(hp_ref[...], sh, sc).astype(BF16)
    un = _modulate(hn_ref[...], sh, sc).astype(BF16)
    row = lax.broadcasted_iota(jnp.int32, (tm, 1), 0)
    outs = []
    for blk in range(3):
        w, b = w_ref[blk], b_ref[blk]
        z = jnp.dot(u, w, preferred_element_type=F32) + b
        zp = jnp.dot(up, w, preferred_element_type=F32) + b
        zn = jnp.dot(un, w, preferred_element_type=F32) + b
        before = jnp.where(i == 0, 0.0, zp[HY_HALO - 1:HY_HALO, :])
        after = jnp.where(i == last, 0.0, zn[0:1, :])
        zm1 = jnp.where(row == 0, before, pltpu.roll(z, 1, axis=0))
        zp1 = jnp.where(row == tm - 1, after, pltpu.roll(z, tm - 1, axis=0))
        outs.append(zm1 * cw_ref[0, blk] + z * cw_ref[1, blk] + zp1 * cw_ref[2, blk] + cb_ref[blk])
    x0_ref[...] = outs[0]
    vx_ref[...] = outs[2] * outs[1]


def hy_in(h, mod, is_ctx, w_in, b_in, conv_w, conv_b):
    bsz, L, d = h.shape
    tm = min(HY_ROWS, L)
    per = tm // HY_HALO
    n_halo = L // HY_HALO
    row = (lambda b, i: (bsz, 0, 0)) if is_ctx else (lambda b, i: (b, 0, 0))
    tile = pl.BlockSpec((None, tm, d), lambda b, i: (b, i, 0))
    shp = jax.ShapeDtypeStruct((bsz, L, d), F32)
    const = lambda n: (lambda b, i: (0,) * n)
    return pl.pallas_call(
        functools.partial(_hy_in_kernel, tm=tm),
        out_shape=(shp, shp),
        grid=(bsz, L // tm),
        in_specs=[tile,
                  pl.BlockSpec((None, HY_HALO, d), lambda b, i: (b, jnp.maximum(i * per - 1, 0), 0)),
                  pl.BlockSpec((None, HY_HALO, d), lambda b, i: (b, jnp.minimum((i + 1) * per, n_halo - 1), 0)),
                  pl.BlockSpec((None, N_MOD, d), row),
                  pl.BlockSpec((3, d, d), const(3), pipeline_mode=pl.Buffered(1)),
                  pl.BlockSpec((3, 1, d), const(3)),
                  pl.BlockSpec((3, 3, 1, d), const(4)),
                  pl.BlockSpec((3, 1, d), const(3))],
        out_specs=(tile, tile),
        compiler_params=_params(2),
        name="hy_in",
    )(h, h, h, mod, w_in.astype(BF16).reshape(d, 3, d).transpose(1, 0, 2), b_in.reshape(3, 1, d),
      conv_w.reshape(3, 3, 1, d), conv_b.reshape(3, 1, d))


def _hy_filter_kernel(band_ref, w1_ref, b1_ref, w2_ref, b2_ref, w3_ref, b3_ref, w4_ref, fr_ref, dl_ref, o_ref,
                      *, tr, L, d):
    n = pl.program_id(0) * tr + lax.broadcasted_iota(jnp.int32, (tr, 1), 0)
    j = jnp.where(n < L, n, 2 * L - n).astype(F32)
    t = j / (L - 1)
    lane = lax.broadcasted_iota(jnp.int32, (1, 128), 1)
    ang = (2 * math.pi * j / L) * band_ref[...]
    feats = jnp.where(lane == 0, t,
                      jnp.where(lane <= HY_BANDS, jnp.cos(ang),
                                jnp.where(lane <= 2 * HY_BANDS, -jnp.sin(ang), 0.0)))
    fr = fr_ref[...]
    a = jnp.sin(fr * (_dot_f32(feats, w1_ref[...]) + b1_ref[...]))
    a = jnp.sin(fr * (_dot_f32(a, w2_ref[...]) + b2_ref[...]))
    a = jnp.sin(fr * (_dot_f32(a, w3_ref[...]) + b3_ref[...]))
    kv = _dot_f32(a, w4_ref[...])
    val = jnp.where(n < L, kv[:, :d], kv[:, d:]) * jnp.exp(-t * dl_ref[...])
    o_ref[...] = jnp.where(n == L, 0.0, val)


def hy_filter(L, w1, b1, w2, b2, w3, b3, w4, sin_freq):
    d = w4.shape[1] // 2
    tr = 256
    band = np.zeros((1, 128), np.float32)
    bands = np.linspace(1e-4, HY_BANDS - 1, HY_BANDS, dtype=np.float32)
    band[0, 1:1 + HY_BANDS] = bands
    band[0, 1 + HY_BANDS:1 + 2 * HY_BANDS] = bands
    w1p = jnp.zeros((128, HY_ORDER), F32).at[:HY_EMB].set(w1)
    deltas = np.abs(np.linspace(math.log(HY_TARGET) / HY_SLOW_PCT, math.log(HY_TARGET) / HY_FAST_PCT, d,
                                dtype=np.float32)).reshape(1, d)
    full = lambda a: pl.BlockSpec(a.shape, lambda i: (0,) * a.ndim)
    args = [jnp.asarray(band), w1p, b1.reshape(1, -1), w2, b2.reshape(1, -1), w3, b3.reshape(1, -1), w4,
            sin_freq.reshape(1, -1), jnp.asarray(deltas)]
    return pl.pallas_call(
        functools.partial(_hy_filter_kernel, tr=tr, L=L, d=d),
        out_shape=jax.ShapeDtypeStruct((2 * L, d), F32),
        grid=(2 * L // tr,),
        in_specs=[full(a) for a in args],
        out_specs=pl.BlockSpec((tr, d), lambda i: (i, 0)),
        compiler_params=_params(1),
        name="hy_filter",
    )(*args)


def _cis(num, den, sign):
    ph = (num % den).astype(F32) * (2 * math.pi / den)
    return jnp.cos(ph), sign * jnp.sin(ph)


def _cblock(re, im):
    return jnp.concatenate([jnp.concatenate([re, -im], axis=-1), jnp.concatenate([im, re], axis=-1)], axis=-2)


def _left_kernel(m_ref, x_ref, o_ref):
    o_ref[...] = jnp.dot(m_ref[...], x_ref[...].astype(BF16), preferred_element_type=F32)


def fft_left(mat, x):
    P, K, lanes = x.shape
    M = mat.shape[0]
    nb = min(FFT_LANES, lanes)
    return pl.pallas_call(
        _left_kernel,
        out_shape=jax.ShapeDtypeStruct((P, M, lanes), F32),
        grid=(P, lanes // nb),
        in_specs=[pl.BlockSpec((M, K), lambda p, j: (0, 0)),
                  pl.BlockSpec((None, K, nb), lambda p, j: (p, 0, j))],
        out_specs=pl.BlockSpec((None, M, nb), lambda p, j: (p, 0, j)),
        compiler_params=_params(2),
        name="fft_left",
    )(mat.astype(BF16), x)


def _mid_kernel(*refs, pairs, n2, with_filter):
    if with_filter:
        a_ref, k_ref, mf_ref, mi_ref, o_ref = refs
    else:
        a_ref, mf_ref, o_ref = refs
    for p in range(pairs):
        a = jnp.concatenate([a_ref[p, 0], a_ref[p, 1]], axis=0).astype(BF16)
        x = jnp.dot(mf_ref[...], a, preferred_element_type=F32)
        xr, xi = x[:n2], x[n2:]
        if with_filter:
            kr, ki = k_ref[0], k_ref[1]
            y = jnp.concatenate([xr * kr - xi * ki, xr * ki + xi * kr], axis=0).astype(BF16)
            x = jnp.dot(mi_ref[...], y, preferred_element_type=F32)
            xr, xi = x[:n2], x[n2:]
        o_ref[p, 0] = xr
        o_ref[p, 1] = xi


def fft_mid(a, mf, kspec=None, mi=None):
    P, _, n1, n2, d = a.shape
    cb = min(FFT_CH, d)
    a_spec = pl.BlockSpec((P, 2, None, n2, cb), lambda c, f: (0, 0, f, 0, c))
    m_spec = pl.BlockSpec((None, 2 * n2, 2 * n2), lambda c, f: (f, 0, 0))
    if kspec is None:
        args, specs = [a, mf], [a_spec, m_spec]
    else:
        args = [a, kspec, mf, mi]
        specs = [a_spec, pl.BlockSpec((2, None, n2, cb), lambda c, f: (0, f, 0, c)), m_spec, m_spec]
    return pl.pallas_call(
        functools.partial(_mid_kernel, pairs=P, n2=n2, with_filter=kspec is not None),
        out_shape=jax.ShapeDtypeStruct(a.shape, F32),
        grid=(d // cb, n1),
        in_specs=specs,
        out_specs=a_spec,
        compiler_params=_params(2),
        name="fft_mid",
    )(*args)


def long_conv_two_stage(vx, kc):
    bsz, L, d = vx.shape
    n = 2 * L
    n2 = FFT_N2
    n1 = n // n2
    P = bsz // 2
    i1 = jnp.arange(n1, dtype=jnp.int32)
    i2 = jnp.arange(n2, dtype=jnp.int32)
    fr, fi = _cis(i1[:, None] * i1[None, :], n1, -1.0)
    ph = i2[None, None, :] * (i1[:, None, None] + n1 * i2[None, :, None])
    er, ei = _cis(ph, n, -1.0)
    mf = _cblock(er, ei).astype(BF16)
    mi = _cblock(er.transpose(0, 2, 1), -ei.transpose(0, 2, 1)).astype(BF16)
    gr, gi = _cis(i1[: n1 // 2, None] * i1[None, :], n1, 1.0)
    g = _cblock(gr, gi) / n
    ak = fft_left(jnp.concatenate([fr, fi], axis=0), kc.reshape(1, n1, n2 * d))
    kspec = fft_mid(ak.reshape(1, 2, n1, n2, d), mf)[0]
    a = fft_left(_cblock(fr[:, : n1 // 2], fi[:, : n1 // 2]), vx.reshape(P, n1, n2 * d))
    bmid = fft_mid(a.reshape(P, 2, n1, n2, d), mf, kspec, mi)
    y = fft_left(g, bmid.reshape(P, 2 * n1, n2 * d))
    return y.reshape(bsz, L, d)


def _direct_conv_kernel(x_ref, k_ref, f_ref, fk_ref, g_ref, o_ref, *, n):
    kf = jnp.dot(fk_ref[...], k_ref[...].astype(BF16), preferred_element_type=F32)
    x = jnp.dot(f_ref[...], x_ref[...].astype(BF16), preferred_element_type=F32)
    xr, xi, kr, ki = x[:n], x[n:], kf[:n], kf[n:]
    y = jnp.concatenate([xr * kr - xi * ki, xr * ki + xi * kr], axis=0).astype(BF16)
    o_ref[...] = jnp.dot(g_ref[...], y, preferred_element_type=F32)


def long_conv_direct(vx, kc):
    bsz, L, d = vx.shape
    n = 2 * L
    P = bsz // 2
    idx = jnp.arange(n, dtype=jnp.int32)
    fr, fi = _cis(idx[:, None] * idx[None, :], n, -1.0)
    f_data = _cblock(fr[:, :L], fi[:, :L]).astype(BF16)
    f_filt = jnp.concatenate([fr, fi], axis=0).astype(BF16)
    g = (_cblock(fr[:L, :], -fi[:L, :]) / n).astype(BF16)
    full = lambda a: pl.BlockSpec(a.shape, lambda p: (0,) * a.ndim)
    y = pl.pallas_call(
        functools.partial(_direct_conv_kernel, n=n),
        out_shape=jax.ShapeDtypeStruct((P, n, d), F32),
        grid=(P,),
        in_specs=[pl.BlockSpec((None, n, d), lambda p: (p, 0, 0)), full(kc), full(f_data), full(f_filt), full(g)],
        out_specs=pl.BlockSpec((None, n, d), lambda p: (p, 0, 0)),
        compiler_params=_params(1),
        name="conv_direct",
    )(vx.reshape(P, n, d), kc, f_data, f_filt, g)
    return y.reshape(bsz, L, d)


def _hy_tail_kernel(h_ref, x0_ref, vx_ref, cv_ref, mod_ref, fb_ref, w_ref, b_ref, o_ref):
    y = (x0_ref[...] * (cv_ref[...] + vx_ref[...] * fb_ref[...])).astype(BF16)
    o_ref[...] = h_ref[...] + mod_ref[5:6, :] * (jnp.dot(y, w_ref[...], preferred_element_type=F32) + b_ref[...])


def hy_tail(h, x0, vx, conv, mod, is_ctx, filt_bias, w_out, b_out):
    bsz, L, d = h.shape
    tm = min(HY_ROWS, L)
    row = (lambda b, i: (bsz, 0, 0)) if is_ctx else (lambda b, i: (b, 0, 0))
    tile = pl.BlockSpec((None, tm, d), lambda b, i: (b, i, 0))
    vec = pl.BlockSpec((1, d), lambda b, i: (0, 0))
    return pl.pallas_call(
        _hy_tail_kernel,
        out_shape=jax.ShapeDtypeStruct((bsz, L, d), F32),
        grid=(bsz, L // tm),
        in_specs=[tile, tile, tile, tile, pl.BlockSpec((None, N_MOD, d), row), vec,
                  pl.BlockSpec((d, d), lambda b, i: (0, 0)), vec],
        out_specs=tile,
        compiler_params=_params(2),
        name="hy_tail",
    )(h, x0, vx, conv, mod, filt_bias.reshape(1, d), w_out.astype(BF16), b_out.reshape(1, d))


def hyena_layer(h, mod, is_ctx, w_in, b_in, conv_w, conv_b, w1, b1, w2, b2, w3, b3, w4, sin_freq, filt_bias,
                w_out, b_out):
    L = h.shape[1]
    kc = hy_filter(L, w1, b1, w2, b2, w3, b3, w4, sin_freq)
    x0, vx = hy_in(h, mod, is_ctx, w_in, b_in, conv_w, conv_b)
    conv = long_conv_direct(vx, kc) if L <= FFT_DIRECT_MAX else long_conv_two_stage(vx, kc)
    return hy_tail(h, x0, vx, conv, mod, is_ctx, filt_bias, w_out, b_out)


def kernel(x, c, ctx, c_ctx, w_ada, b_ada, ffn_w_gate, ffn_w_up, ffn_w_down, hg_w_in, hg_lb_logits, hg_norm_gain,
           hg_w_out, pool_w, pool_scale, hy_w_in, hy_b_in, hy_conv_w, hy_conv_b, hy_w1, hy_b1, hy_w2, hy_b2, hy_w3,
           hy_b3, hy_w4, hy_sin_freq, hy_filt_bias, hy_w_out, hy_b_out, final_gain):
    depth = w_ada.shape[0]
    mods = ada_table(c, c_ctx, w_ada, b_ada)
    h, hc = x, ctx
    for i in range(depth):
        kind, j = i % 3, i // 3
        last = i == depth - 1
        ctx_live = (not last) or kind == 0
        mod = mods[i]
        ffn = lambda a, is_ctx, k0, s, gain=None: ffn_half_step(
            a, mod, is_ctx, k0, ffn_w_gate[i, s], ffn_w_up[i, s], ffn_w_down[i, s], gain)
        h = ffn(h, False, 0, 0)
        if ctx_live:
            hc = ffn(hc, True, 0, 0)
        if kind == 0:
            h, hc = hgrn2_layer(h, hc, mod, hg_w_in[j], hg_lb_logits, j, hg_norm_gain[j], hg_w_out[j], not last)
        elif kind == 1:
            h = pool_layer(h, mod, False, pool_w[j], pool_scale[j])
            if not last:
                hc = pool_layer(hc, mod, True, pool_w[j], pool_scale[j])
        else:
            hy = (hy_w_in[j], hy_b_in[j], hy_conv_w[j], hy_conv_b[j], hy_w1[j], hy_b1[j], hy_w2[j], hy_b2[j],
                  hy_w3[j], hy_b3[j], hy_w4[j], hy_sin_freq[j], hy_filt_bias[j], hy_w_out[j], hy_b_out[j])
            h = hyena_layer(h, mod, False, *hy)
            if not last:
                hc = hyena_layer(hc, mod, True, *hy)
        h = ffn(h, False, 6, 1, final_gain if last else None)
        if not last:
            hc = ffn(hc, True, 6, 1)
    return h
```

```python
import functools
import math

import jax
import jax.numpy as jnp
import numpy as np
from jax import lax
from jax.experimental import pallas as pl
from jax.experimental.pallas import tpu as pltpu

F32 = jnp.float32
BF16 = jnp.bfloat16

D_MODEL = 1024
DEPTH = 4
N_MOD = 9
D_FF = 2816
EPS = 1e-6
GRID_W = 64

HG_HEADS = 8
HG_DK = 128

POOL_WINDOWS = (2, 4, 8, 16)
POOL_GC = D_MODEL // len(POOL_WINDOWS)

HY_EMB = 33
HY_BANDS = (HY_EMB - 1) // 2
HY_ORDER = 64
HY_TARGET = 1e-2
HY_FAST_PCT = 0.3
HY_SLOW_PCT = 1.5

MOD_ROWS = 8
V7X_VMEM_LIMIT = 56 * 1024 * 1024


def _params(n_grid, vmem=None):
    return pltpu.CompilerParams(dimension_semantics=("arbitrary",) * n_grid,
                                vmem_limit_bytes=vmem or V7X_VMEM_LIMIT)


def _silu(x):
    return x * jax.nn.sigmoid(x)


def _rms(x):
    return x * lax.rsqrt(jnp.mean(x * x, axis=-1, keepdims=True) + EPS)


def _modulate(x, shift, scale):
    return _rms(x) * (1.0 + scale) + shift


def _ada_kernel(cc_ref, w_ref, b_ref, o_ref):
    s = _silu(cc_ref[...]).astype(BF16)
    o_ref[...] = jnp.dot(s, w_ref[...].astype(BF16), preferred_element_type=F32) + b_ref[...]


def ada_table(c, c_ctx, w_ada, b_ada):
    bsz, d = c.shape
    depth, _, nd = w_ada.shape
    cc = jnp.zeros((MOD_ROWS, d), F32).at[:bsz].set(c).at[bsz].set(c_ctx)
    tn = 1024
    out = pl.pallas_call(
        _ada_kernel,
        out_shape=jax.ShapeDtypeStruct((depth, MOD_ROWS, nd), F32),
        grid=(depth, nd // tn),
        in_specs=[pl.BlockSpec((MOD_ROWS, d), lambda l, j: (0, 0)),
                  pl.BlockSpec((None, d, tn), lambda l, j: (l, 0, j)),
                  pl.BlockSpec((None, 1, tn), lambda l, j: (l, 0, j))],
        out_specs=pl.BlockSpec((None, MOD_ROWS, tn), lambda l, j: (l, 0, j)),
        compiler_params=_params(2),
        name="ada_table",
    )(cc, w_ada, b_ada.reshape(depth, 1, nd))
    return out.reshape(depth, MOD_ROWS, N_MOD, d)


FFN_CHUNK = 256
FFN_ROWS = 512


def _ffn_kernel(h_ref, mod_ref, wg_ref, wu_ref, wd_ref, *rest, k0, n_chunks, final):
    if final:
        gain_ref, o_ref, acc_ref = rest
    else:
        o_ref, acc_ref = rest
    x = h_ref[...]
    y = _modulate(x, mod_ref[k0:k0 + 1, :], mod_ref[k0 + 1:k0 + 2, :]).astype(BF16)
    acc_ref[...] = jnp.zeros_like(acc_ref)

    def hidden(j):
        g = jnp.dot(y, wg_ref[j], preferred_element_type=F32)
        u = jnp.dot(y, wu_ref[j], preferred_element_type=F32)
        return (_silu(g) * u).astype(BF16)

    def body(j, a):
        a_next = hidden(j + 1)
        acc_ref[...] += jnp.dot(a, wd_ref[j], preferred_element_type=F32)
        return a_next

    a_last = lax.fori_loop(0, n_chunks - 1, body, hidden(0), unroll=True)
    acc_ref[...] += jnp.dot(a_last, wd_ref[n_chunks - 1], preferred_element_type=F32)
    out = x + (0.5 * mod_ref[k0 + 2:k0 + 3, :]) * acc_ref[...]
    if final:
        out = _rms(out) * gain_ref[...]
    o_ref[...] = out


def _chunk_cols(w):
    d, f = w.shape
    return w.astype(BF16).reshape(d, f // FFN_CHUNK, FFN_CHUNK).transpose(1, 0, 2)


def ffn_half_step(h, mod, is_ctx, k0, w_gate, w_up, w_down, final_gain=None):
    bsz, L, d = h.shape
    f = w_gate.shape[1]
    n_chunks = f // FFN_CHUNK
    tm = min(FFN_ROWS, L)
    wg = _chunk_cols(w_gate)
    wu = _chunk_cols(w_up)
    wd = w_down.astype(BF16).reshape(n_chunks, FFN_CHUNK, d)
    row = (lambda b, i: (bsz, 0, 0)) if is_ctx else (lambda b, i: (b, 0, 0))
    const3 = lambda b, i: (0, 0, 0)
    single = pl.Buffered(1)
    in_specs = [pl.BlockSpec((None, tm, d), lambda b, i: (b, i, 0)),
                pl.BlockSpec((None, N_MOD, d), row),
                pl.BlockSpec((n_chunks, d, FFN_CHUNK), const3, pipeline_mode=single),
                pl.BlockSpec((n_chunks, d, FFN_CHUNK), const3, pipeline_mode=single),
                pl.BlockSpec((n_chunks, FFN_CHUNK, d), const3, pipeline_mode=single)]
    args = [h, mod, wg, wu, wd]
    if final_gain is not None:
        in_specs.append(pl.BlockSpec((1, d), lambda b, i: (0, 0)))
        args.append(final_gain.reshape(1, d))
    return pl.pallas_call(
        functools.partial(_ffn_kernel, k0=k0, n_chunks=n_chunks, final=final_gain is not None),
        out_shape=jax.ShapeDtypeStruct((bsz, L, d), F32),
        grid=(bsz, L // tm),
        in_specs=in_specs,
        out_specs=pl.BlockSpec((None, tm, d), lambda b, i: (b, i, 0)),
        scratch_shapes=[pltpu.VMEM((tm, d), F32)],
        compiler_params=_params(2),
        name="ffn_half_step",
    )(*args)


HG_ROWS = 256
GLA_TILE = 256
GLA_BLOCK = 32
GLA_SAFE_LOG_DECAY = 80.0


def _split3(x):
    x1 = x.astype(BF16)
    r1 = x - x1.astype(F32)
    x2 = r1.astype(BF16)
    x3 = (r1 - x2.astype(F32)).astype(BF16)
    return x1, x2, x3


def _dot_exact_lhs(a_bf16, x):
    x1, x2, x3 = _split3(x)
    return (jnp.dot(a_bf16, x1, preferred_element_type=F32)
            + jnp.dot(a_bf16, x2, preferred_element_type=F32)
            + jnp.dot(a_bf16, x3, preferred_element_type=F32))


def _hg_proj_kernel(h_ref, mod_ref, w_ref, lbl_ref, q_ref, kf_ref, gf_ref, kb_ref, gb_ref, v_ref, sg_ref, *, layer_j):
    x = h_ref[...]
    u = _modulate(x, mod_ref[3:4, :], mod_ref[4:5, :]).astype(BF16)
    q_ref[...] = _silu(jnp.dot(u, w_ref[0], preferred_element_type=F32)).astype(q_ref.dtype)
    v_ref[...] = jnp.dot(u, w_ref[3], preferred_element_type=F32).astype(v_ref.dtype)
    sg_ref[...] = _silu(jnp.dot(u, w_ref[4], preferred_element_type=F32)).astype(sg_ref.dtype)
    for d, (k_ref, g_ref) in enumerate(((kf_ref, gf_ref), (kb_ref, gb_ref))):
        logits = lbl_ref[d]
        e = jnp.exp(logits - jnp.max(logits, axis=0, keepdims=True))
        p = e / jnp.sum(e, axis=0, keepdims=True)
        lb = jnp.sum(p[:layer_j + 1], axis=0, keepdims=True) - p[0:1]
        z = jnp.dot(u, w_ref[1 + d], preferred_element_type=F32)
        g_ref[...] = jnp.log(lb + (1.0 - lb) * jax.nn.sigmoid(z))
        k_ref[...] = (1.0 - lb) * jax.nn.sigmoid(-z)


def hg_project(h, mod, is_ctx, w_in, lb_logits, layer_j):
    bsz, L, d = h.shape
    tm = min(HG_ROWS, L)
    n_blk = w_in.shape[1] // d
    w = w_in.astype(BF16).reshape(d, n_blk, d).transpose(1, 0, 2)
    row = (lambda b, i: (bsz, 0, 0)) if is_ctx else (lambda b, i: (b, 0, 0))
    tile = pl.BlockSpec((None, tm, d), lambda b, i: (b, i, 0))
    shp = jax.ShapeDtypeStruct((bsz, L, d), F32)
    half = jax.ShapeDtypeStruct((bsz, L, d), BF16)
    return pl.pallas_call(
        functools.partial(_hg_proj_kernel, layer_j=layer_j),
        out_shape=(half, shp, shp, shp, shp, half, half),
        grid=(bsz, L // tm),
        in_specs=[tile,
                  pl.BlockSpec((None, N_MOD, d), row),
                  pl.BlockSpec((n_blk, d, d), lambda b, i: (0, 0, 0), pipeline_mode=pl.Buffered(1)),
                  pl.BlockSpec(lb_logits.shape, lambda b, i: (0, 0, 0))],
        out_specs=(tile,) * 7,
        compiler_params=_params(2),
        name="hg_project",
    )(h, mod, w, lb_logits)


def _gla_consts(tl, blk):
    r = np.arange(tl)
    same = (r[:, None] // blk) == (r[None, :] // blk)
    tri_f = same & (r[None, :] <= r[:, None])
    tri_b = same & (r[None, :] >= r[:, None])
    return (jnp.asarray(np.stack([tri_f, tri_b]), BF16), jnp.asarray(same, BF16))


def _gla_kernel(qf_ref, qb_ref, kf_ref, gf_ref, kb_ref, gb_ref, vf_ref, vb_ref, tri_ref, ones_ref, s0_ref,
                of_ref, ob_ref, sfin_ref,
                st_ref, bc_ref, qd_ref, kt_ref, ke_ref, dec_ref, *, n_blk, blk, heads, dk):
    i = pl.program_id(1)

    @pl.when(i == 0)
    def _():
        st_ref[...] = s0_ref[...]

    dirs = ((qf_ref, kf_ref, gf_ref, vf_ref, of_ref), (qb_ref, kb_ref, gb_ref, vb_ref, ob_ref))
    worst = None
    for d, (q_ref, k_ref, g_ref, v_ref, _) in enumerate(dirs):
        g = g_ref[...]
        bc = _dot_exact_lhs(tri_ref[d], g)
        bl = _dot_exact_lhs(ones_ref[...], g)
        bc_ref[d] = bc
        dec_ref[d] = jnp.exp(bl)
        qd_ref[d] = (q_ref[...] * jnp.exp(bc)).astype(BF16)
        ke_ref[d] = (k_ref[...] * jnp.exp(bl - bc)).astype(BF16)
        m = jnp.min(bl)
        worst = m if worst is None else jnp.minimum(worst, m)
    safe = worst >= -GLA_SAFE_LOG_DECAY

    @pl.when(safe)
    def _():
        for d, (_, k_ref, _, _, _) in enumerate(dirs):
            kt_ref[d] = (k_ref[...] * jnp.exp(-bc_ref[d])).astype(BF16)

    t_io = lax.broadcasted_iota(jnp.int32, (blk, blk), 0)
    s_io = lax.broadcasted_iota(jnp.int32, (blk, blk), 1)
    t_io_w = lax.broadcasted_iota(jnp.int32, (blk, dk), 0)
    sub_io = lax.broadcasted_iota(jnp.int32, (8, dk), 0)
    nt =(((1,), (1,)), ((), ()))
    tn = (((0,), (0,)), ((), ()))

    def run(fast):
        def step(jb, carry):
            chains = []
            for d in range(2):
                r0 = pl.multiple_of((jb if d == 0 else n_blk - 1 - jb) * blk, blk)
                for hh in range(heads):
                    chains.append((d, hh, r0, pl.ds(r0, blk), slice(hh * dk, (hh + 1) * dk)))
            scores, carried = [], []
            for d, hh, r0, rows, cols in chains:
                q_ref, k_ref = dirs[d][0], dirs[d][1]
                qd = qd_ref[d, rows, cols]
                if fast:
                    sc = lax.dot_general(qd, kt_ref[d, rows, cols], nt, preferred_element_type=F32)
                    sc = jnp.where((s_io <= t_io) if d == 0 else (s_io >= t_io), sc, 0.0)
                else:
                    qq = q_ref[rows, cols]
                    bb = bc_ref[d, rows, cols]

                    def col(s, sc, d=d, r0=r0, cols=cols, qq=qq, bb=bb, k_ref=k_ref):
                        grp = pl.ds(pl.multiple_of(r0 + (s // 8) * 8, 8), 8)
                        pick = sub_io == s % 8
                        brow = jnp.sum(jnp.where(pick, bc_ref[d, grp, cols], 0.0), axis=0, keepdims=True)
                        krow = jnp.sum(jnp.where(pick, k_ref[grp, cols], 0.0), axis=0, keepdims=True)
                        ok = (t_io_w >= s) if d == 0 else (t_io_w <= s)
                        p = jnp.where(ok, qq * krow * jnp.exp(jnp.minimum(bb - brow, 0.0)), 0.0)
                        return jnp.where(s_io == s, jnp.sum(p, axis=-1, keepdims=True), sc)

                    sc = lax.fori_loop(0, blk, col, jnp.zeros((blk, blk), F32))
                scores.append(sc.astype(BF16))
                carried.append(lax.dot_general(qd, st_ref[d, hh].astype(BF16), nt, preferred_element_type=F32))
            for (d, hh, r0, rows, cols), sc, from_state in zip(chains, scores, carried):
                v_ref, o_ref = dirs[d][3], dirs[d][4]
                o_ref[rows, cols] = jnp.dot(sc, v_ref[rows, cols], preferred_element_type=F32) + from_state
            for d, hh, r0, rows, cols in chains:
                dec = dec_ref[d, pl.ds(r0, 1), cols]
                st_ref[d, hh] = st_ref[d, hh] * dec + lax.dot_general(
                    dirs[d][3][rows, cols], ke_ref[d, rows, cols], tn, preferred_element_type=F32)
            return carry

        lax.fori_loop(0, n_blk, step, 0, unroll=fast)

    @pl.when(safe)
    def _():
        run(True)

    @pl.when(jnp.logical_not(safe))
    def _():
        run(False)

    @pl.when(i == pl.num_programs(1) - 1)
    def _():
        sfin_ref[...] = st_ref[...]


def gla_bidirectional(q, k_f, g_f, k_b, g_b, v, s0):
    bsz, L, d = q.shape
    tl = min(GLA_TILE, L)
    nt = L // tl
    n_blk = tl // GLA_BLOCK
    tri, ones = _gla_consts(tl, GLA_BLOCK)
    fwd = pl.BlockSpec((None, tl, d), lambda b, i: (b, i, 0))
    bwd = pl.BlockSpec((None, tl, d), lambda b, i: (b, nt - 1 - i, 0))
    st_spec = pl.BlockSpec((2, None, HG_HEADS, HG_DK, HG_DK), lambda b, i: (0, b, 0, 0, 0))
    shp = jax.ShapeDtypeStruct((bsz, L, d), F32)
    return pl.pallas_call(
        functools.partial(_gla_kernel, n_blk=n_blk, blk=GLA_BLOCK, heads=HG_HEADS, dk=HG_DK),
        out_shape=(shp, shp, jax.ShapeDtypeStruct(s0.shape, F32)),
        grid=(bsz, nt),
        in_specs=[fwd, bwd, fwd, fwd, bwd, bwd, fwd, bwd,
                  pl.BlockSpec((2, tl, tl), lambda b, i: (0, 0, 0)),
                  pl.BlockSpec((tl, tl), lambda b, i: (0, 0)),
                  st_spec],
        out_specs=(fwd, bwd, st_spec),
        scratch_shapes=[pltpu.VMEM((2, HG_HEADS, HG_DK, HG_DK), F32),
                        pltpu.VMEM((2, tl, d), F32),
                        pltpu.VMEM((2, tl, d), BF16),
                        pltpu.VMEM((2, tl, d), BF16),
                        pltpu.VMEM((2, tl, d), BF16),
                        pltpu.VMEM((2, tl, d), F32)],
        compiler_params=_params(2),
        name="gla_bidirectional",
    )(q, q, k_f, g_f, k_b, g_b, v, v, tri, ones, s0)


def _hg_readout_kernel(h_ref, of_ref, ob_ref, sg_ref, mod_ref, gain_ref, w_ref, o_ref, *, heads, dk):
    o = of_ref[...] + ob_ref[...]
    parts = []
    for hh in range(heads):
        parts.append(_rms(o[:, hh * dk:(hh + 1) * dk]))
    on = jnp.concatenate(parts, axis=-1)
    y = (on * gain_ref[...] * sg_ref[...]).astype(BF16)
    o_ref[...] = h_ref[...] + mod_ref[5:6, :] * jnp.dot(y, w_ref[...], preferred_element_type=F32)


def hg_readout(h, o_f, o_b, sg, mod, is_ctx, norm_gain, w_out):
    bsz, L, d = h.shape
    tm = min(HG_ROWS, L)
    row = (lambda b, i: (bsz, 0, 0)) if is_ctx else (lambda b, i: (b, 0, 0))
    tile = pl.BlockSpec((None, tm, d), lambda b, i: (b, i, 0))
    return pl.pallas_call(
        functools.partial(_hg_readout_kernel, heads=HG_HEADS, dk=HG_DK),
        out_shape=jax.ShapeDtypeStruct((bsz, L, d), F32),
        grid=(bsz, L // tm),
        in_specs=[tile, tile, tile, tile,
                  pl.BlockSpec((None, N_MOD, d), row),
                  pl.BlockSpec((1, d), lambda b, i: (0, 0)),
                  pl.BlockSpec((d, d), lambda b, i: (0, 0))],
        out_specs=tile,
        compiler_params=_params(2),
        name="hg_readout",
    )(h, o_f, o_b, sg, mod, norm_gain.reshape(1, d), w_out.astype(BF16))


def hgrn2_layer(h, hc, mod, w_in, lb_logits, layer_j, norm_gain, w_out, ctx_out):
    bsz = h.shape[0]
    qc, kfc, gfc, kbc, gbc, vc, sgc = hg_project(hc, mod, True, w_in, lb_logits, layer_j)
    ql, kfl, gfl, kbl, gbl, vl, sgl = hg_project(h, mod, False, w_in, lb_logits, layer_j)
    s0 = jnp.zeros((2, bsz, HG_HEADS, HG_DK, HG_DK), F32)
    ocf, ocb, s_ctx = gla_bidirectional(qc, kfc, gfc, kbc, gbc, vc, s0)
    olf, olb, _ = gla_bidirectional(ql, kfl, gfl, kbl, gbl, vl, s_ctx)
    h = hg_readout(h, olf, olb, sgl, mod, False, norm_gain, w_out)
    if ctx_out:
        hc = hg_readout(hc, ocf, ocb, sgc, mod, True, norm_gain, w_out)
    return h, hc


POOL_ROWS = 256


def _window_matrix(n, w, period=None):
    p = np.arange(n)
    period = period or n
    lo, hi = p - w // 2, p + (w - w // 2)
    same = (p[:, None] // period) == (p[None, :] // period)
    return same & (p[None, :] >= lo[:, None]) & (p[None, :] < hi[:, None])


def _window_count(pos, n, w):
    return (jnp.clip(pos + (w - w // 2), 0, n) - jnp.clip(pos - w // 2, 0, n)).astype(F32)


def _pool_row_kernel(h_ref, mod_ref, a_ref, o_ref, *, n_rows):
    u = _modulate(h_ref[...], mod_ref[3:4, :], mod_ref[4:5, :])
    r = lax.broadcasted_iota(jnp.int32, (n_rows, 1), 0)
    for gi, w in enumerate(POOL_WINDOWS):
        cols = slice(gi * POOL_GC, (gi + 1) * POOL_GC)
        o_ref[:, cols] = _dot_exact_lhs(a_ref[gi], u[:, cols]) / _window_count(r, n_rows, w)


def _pool_main_kernel(*refs, period, on_grid):
    if on_grid:
        h_ref, r_ref, mod_ref, a_ref, wp_ref, sc_ref, o_ref = refs
    else:
        h_ref, mod_ref, a_ref, wp_ref, sc_ref, o_ref = refs
    x = h_ref[...]
    u = _modulate(x, mod_ref[3:4, :], mod_ref[4:5, :])
    src = r_ref[...] if on_grid else u
    pos = lax.broadcasted_iota(jnp.int32, (x.shape[0], 1), 0) % period
    for gi, w in enumerate(POOL_WINDOWS):
        cols = slice(gi * POOL_GC, (gi + 1) * POOL_GC)
        mean = _dot_exact_lhs(a_ref[gi], src[:, cols]) / _window_count(pos, period, w)
        pooled = (mean - u[:, cols]).astype(BF16)
        y = jnp.dot(pooled, wp_ref[gi], preferred_element_type=F32) * sc_ref[:, cols]
        o_ref[:, cols] = x[:, cols] + mod_ref[5:6, cols] * y


def pool_layer(h, mod, is_ctx, w_pool, scale):
    bsz, L, d = h.shape
    wp = w_pool.astype(BF16)
    row = (lambda b, i: (bsz, 0, 0)) if is_ctx else (lambda b, i: (b, 0, 0))
    mod_spec = pl.BlockSpec((None, N_MOD, d), row)
    tm = min(POOL_ROWS, L)
    tile = pl.BlockSpec((None, tm, d), lambda b, i: (b, i, 0))
    ng = len(POOL_WINDOWS)
    args, specs = [h], [tile]
    if is_ctx:
        assert tm == L, "the 1D windows need the whole context in one tile"
        period = L
    else:
        n_rows = L // GRID_W
        a_row = jnp.asarray(np.stack([_window_matrix(n_rows, w) for w in POOL_WINDOWS]), BF16)
        rows = pl.pallas_call(
            functools.partial(_pool_row_kernel, n_rows=n_rows),
            out_shape=jax.ShapeDtypeStruct((bsz, n_rows, GRID_W * d), F32),
            grid=(bsz, GRID_W),
            in_specs=[pl.BlockSpec((None, n_rows, d), lambda b, c: (b, 0, c)),
                      pl.BlockSpec((None, N_MOD, d), lambda b, c: (b, 0, 0)),
                      pl.BlockSpec((ng, n_rows, n_rows), lambda b, c: (0, 0, 0))],
            out_specs=pl.BlockSpec((None, n_rows, d), lambda b, c: (b, 0, c)),
            compiler_params=_params(2),
            name="pool_rows",
        )(h.reshape(bsz, n_rows, GRID_W * d), mod, a_row)
        args.append(rows.reshape(bsz, L, d))
        specs.append(tile)
        period = GRID_W
    a_col = jnp.asarray(np.stack([_window_matrix(tm, w, period) for w in POOL_WINDOWS]), BF16)
    args += [mod, a_col, wp, scale.reshape(1, d)]
    specs += [mod_spec,
              pl.BlockSpec((ng, tm, tm), lambda b, i: (0, 0, 0)),
              pl.BlockSpec((ng, POOL_GC, POOL_GC), lambda b, i: (0, 0, 0)),
              pl.BlockSpec((1, d), lambda b, i: (0, 0))]
    return pl.pallas_call(
        functools.partial(_pool_main_kernel, period=period, on_grid=not is_ctx),
        out_shape=jax.ShapeDtypeStruct((bsz, L, d), F32),
        grid=(bsz, L // tm),
        in_specs=specs,
        out_specs=tile,
        compiler_params=_params(2),
        name="pool_main",
    )(*args)


HY_ROWS = 256
HY_HALO = 8
FFT_N2 = 128
FFT_LANES = 2048
FFT_CH = 512
FFT_DIRECT_MAX = 512


def _dot_f32(a, b):
    a1, a2, a3 = _split3(a)
    b1, b2, b3 = _split3(b)
    d = lambda x, y: jnp.dot(x, y, preferred_element_type=F32)
    return d(a1, b1) + (d(a1, b2) + d(a2, b1)) + (d(a1, b3) + d(a2, b2) + d(a3, b1))


def _hy_in_kernel(h_ref, hp_ref, hn_ref, mod_ref, w_ref, b_ref, cw_ref, cb_ref, x0_ref, vx_ref, *, tm):
    i = pl.program_id(1)
    last = pl.num_programs(1) - 1
    sh, sc = mod_ref[3:4, :], mod_ref[4:5, :]
    u = jnp.concatenate([_modulate(hp_ref[...], sh, sc), _modulate(h_ref[...], sh, sc),
                         _modulate(hn_ref[...], sh, sc)], axis=0).astype(BF16)
    row = lax.broadcasted_iota(jnp.int32, (tm, 1), 0)
    outs = []
    for blk in range(3):
        zz = jnp.dot(u, w_ref[blk], preferred_element_type=F32) + b_ref[blk]
        z = zz[HY_HALO:HY_HALO + tm]
        before = jnp.where(i == 0, 0.0, zz[HY_HALO - 1:HY_HALO, :])
        after = jnp.where(i == last, 0.0, zz[HY_HALO + tm:HY_HALO + tm + 1, :])
        zm1 = jnp.where(row == 0, before, pltpu.roll(z, 1, axis=0))
        zp1 = jnp.where(row == tm - 1, after, pltpu.roll(z, tm - 1, axis=0))
        outs.append(zm1 * cw_ref[0, blk] + z * cw_ref[1, blk] + zp1 * cw_ref[2, blk] + cb_ref[blk])
    x0_ref[...] = outs[0]
    vx_ref[...] = outs[2] * outs[1]


def hy_in(h, mod, is_ctx, w_in, b_in, conv_w, conv_b):
    bsz, L, d = h.shape
    tm = min(HY_ROWS, L)
    per = tm // HY_HALO
    n_halo = L // HY_HALO
    row = (lambda b, i: (bsz, 0, 0)) if is_ctx else (lambda b, i: (b, 0, 0))
    tile = pl.BlockSpec((None, tm, d), lambda b, i: (b, i, 0))
    shp = jax.ShapeDtypeStruct((bsz, L, d), F32)
    const = lambda n: (lambda b, i: (0,) * n)
    return pl.pallas_call(
        functools.partial(_hy_in_kernel, tm=tm),
        out_shape=(shp, shp),
        grid=(bsz, L // tm),
        in_specs=[tile,
                  pl.BlockSpec((None, HY_HALO, d), lambda b, i: (b, jnp.maximum(i * per - 1, 0), 0)),
                  pl.BlockSpec((None, HY_HALO, d), lambda b, i: (b, jnp.minimum((i + 1) * per, n_halo - 1), 0)),
                  pl.BlockSpec((None, N_MOD, d), row),
                  pl.BlockSpec((3, d, d), const(3), pipeline_mode=pl.Buffered(1)),
                  pl.BlockSpec((3, 1, d), const(3)),
                  pl.BlockSpec((3, 3, 1, d), const(4)),
                  pl.BlockSpec((3, 1, d), const(3))],
        out_specs=(tile, tile),
        compiler_params=_params(2),
        name="hy_in",
    )(h, h, h, mod, w_in.astype(BF16).reshape(d, 3, d).transpose(1, 0, 2), b_in.reshape(3, 1, d),
      conv_w.reshape(3, 3, 1, d), conv_b.reshape(3, 1, d))


def _hy_filter_kernel(band_ref, w1_ref, b1_ref, w2_ref, b2_ref, w3_ref, b3_ref, w4_ref, fr_ref, dl_ref, o_ref,
                      *, tr, L, d):
    n = pl.program_id(0) * tr + lax.broadcasted_iota(jnp.int32, (tr, 1), 0)
    j = jnp.where(n < L, n, 2 * L - n).astype(F32)
    t = j / (L - 1)
    lane = lax.broadcasted_iota(jnp.int32, (1, 128), 1)
    ang = (2 * math.pi * j / L) * band_ref[...]
    feats = jnp.where(lane == 0, t,
                      jnp.where(lane <= HY_BANDS, jnp.cos(ang),
                                jnp.where(lane <= 2 * HY_BANDS, -jnp.sin(ang), 0.0)))
    fr = fr_ref[...]
    a = jnp.sin(fr * (_dot_f32(feats, w1_ref[...]) + b1_ref[...]))
    a = jnp.sin(fr * (_dot_f32(a, w2_ref[...]) + b2_ref[...]))
    a = jnp.sin(fr * (_dot_f32(a, w3_ref[...]) + b3_ref[...]))
    kv = _dot_f32(a, w4_ref[...])
    val = jnp.where(n < L, kv[:, :d], kv[:, d:]) * jnp.exp(-t * dl_ref[...])
    o_ref[...] = jnp.where(n == L, 0.0, val)


def hy_filter(L, w1, b1, w2, b2, w3, b3, w4, sin_freq):
    d = w4.shape[1] // 2
    tr = 256
    band = np.zeros((1, 128), np.float32)
    bands = np.linspace(1e-4, HY_BANDS - 1, HY_BANDS, dtype=np.float32)
    band[0, 1:1 + HY_BANDS] = bands
    band[0, 1 + HY_BANDS:1 + 2 * HY_BANDS] = bands
    w1p = jnp.zeros((128, HY_ORDER), F32).at[:HY_EMB].set(w1)
    deltas = np.abs(np.linspace(math.log(HY_TARGET) / HY_SLOW_PCT, math.log(HY_TARGET) / HY_FAST_PCT, d,
                                dtype=np.float32)).reshape(1, d)
    full = lambda a: pl.BlockSpec(a.shape, lambda i: (0,) * a.ndim)
    args = [jnp.asarray(band), w1p, b1.reshape(1, -1), w2, b2.reshape(1, -1), w3, b3.reshape(1, -1), w4,
            sin_freq.reshape(1, -1), jnp.asarray(deltas)]
    return pl.pallas_call(
        functools.partial(_hy_filter_kernel, tr=tr, L=L, d=d),
        out_shape=jax.ShapeDtypeStruct((2 * L, d), F32),
        grid=(2 * L // tr,),
        in_specs=[full(a) for a in args],
        out_specs=pl.BlockSpec((tr, d), lambda i: (i, 0)),
        compiler_params=_params(1),
        name="hy_filter",
    )(*args)


def _cis(num, den, sign):
    ph = (num % den).astype(F32) * (2 * math.pi / den)
    return jnp.cos(ph), sign * jnp.sin(ph)


def _cblock(re, im):
    return jnp.concatenate([jnp.concatenate([re, -im], axis=-1), jnp.concatenate([im, re], axis=-1)], axis=-2)


def _left_kernel(m_ref, x_ref, o_ref):
    o_ref[...] = jnp.dot(m_ref[...], x_ref[...].astype(BF16), preferred_element_type=F32)


def fft_left(mat, x):
    P, K, lanes = x.shape
    M = mat.shape[0]
    nb = min(FFT_LANES, lanes)
    return pl.pallas_call(
        _left_kernel,
        out_shape=jax.ShapeDtypeStruct((P, M, lanes), F32),
        grid=(P, lanes // nb),
        in_specs=[pl.BlockSpec((M, K), lambda p, j: (0, 0)),
                  pl.BlockSpec((None, K, nb), lambda p, j: (p, 0, j))],
        out_specs=pl.BlockSpec((None, M, nb), lambda p, j: (p, 0, j)),
        compiler_params=_params(2),
        name="fft_left",
    )(mat.astype(BF16), x)


def _mid_kernel(*refs, pairs, n2, with_filter):
    if with_filter:
        a_ref, k_ref, mf_ref, mi_ref, o_ref = refs
    else:
        a_ref, mf_ref, o_ref = refs
    for p in range(pairs):
        a = jnp.concatenate([a_ref[p, 0], a_ref[p, 1]], axis=0).astype(BF16)
        x = jnp.dot(mf_ref[...], a, preferred_element_type=F32)
        xr, xi = x[:n2], x[n2:]
        if with_filter:
            kr, ki = k_ref[0], k_ref[1]
            y = jnp.concatenate([xr * kr - xi * ki, xr * ki + xi * kr], axis=0).astype(BF16)
            x = jnp.dot(mi_ref[...], y, preferred_element_type=F32)
            xr, xi = x[:n2], x[n2:]
        o_ref[p, 0] = xr
        o_ref[p, 1] = xi


def fft_mid(a, mf, kspec=None, mi=None):
    P, _, n1, n2, d = a.shape
    cb = min(FFT_CH, d)
    a_spec = pl.BlockSpec((P, 2, None, n2, cb), lambda c, f: (0, 0, f, 0, c))
    m_spec = pl.BlockSpec((None, 2 * n2, 2 * n2), lambda c, f: (f, 0, 0))
    if kspec is None:
        args, specs = [a, mf], [a_spec, m_spec]
    else:
        args = [a, kspec, mf, mi]
        specs = [a_spec, pl.BlockSpec((2, None, n2, cb), lambda c, f: (0, f, 0, c)), m_spec, m_spec]
    return pl.pallas_call(
        functools.partial(_mid_kernel, pairs=P, n2=n2, with_filter=kspec is not None),
        out_shape=jax.ShapeDtypeStruct(a.shape, F32),
        grid=(d // cb, n1),
        in_specs=specs,
        out_specs=a_spec,
        compiler_params=_params(2),
        name="fft_mid",
    )(*args)


def long_conv_two_stage(vx, kc):
    bsz, L, d = vx.shape
    n = 2 * L
    n2 = FFT_N2
    n1 = n // n2
    P = bsz // 2
    i1 = jnp.arange(n1, dtype=jnp.int32)
    i2 = jnp.arange(n2, dtype=jnp.int32)
    fr, fi = _cis(i1[:, None] * i1[None, :], n1, -1.0)
    ph = i2[None, None, :] * (i1[:, None, None] + n1 * i2[None, :, None])
    er, ei = _cis(ph, n, -1.0)
    mf = _cblock(er, ei).astype(BF16)
    mi = _cblock(er.transpose(0, 2, 1), -ei.transpose(0, 2, 1)).astype(BF16)
    gr, gi = _cis(i1[: n1 // 2, None] * i1[None, :], n1, 1.0)
    g = _cblock(gr, gi) / n
    ak = fft_left(jnp.concatenate([fr, fi], axis=0), kc.reshape(1, n1, n2 * d))
    kspec = fft_mid(ak.reshape(1, 2, n1, n2, d), mf)[0]
    a = fft_left(_cblock(fr[:, : n1 // 2], fi[:, : n1 // 2]), vx.reshape(P, n1, n2 * d))
    bmid = fft_mid(a.reshape(P, 2, n1, n2, d), mf, kspec, mi)
    y = fft_left(g, bmid.reshape(P, 2 * n1, n2 * d))
    return y.reshape(bsz, L, d)


def _direct_conv_kernel(x_ref, k_ref, f_ref, fk_ref, g_ref, o_ref, *, n):
    kf = jnp.dot(fk_ref[...], k_ref[...].astype(BF16), preferred_element_type=F32)
    x = jnp.dot(f_ref[...], x_ref[...].astype(BF16), preferred_element_type=F32)
    xr, xi, kr, ki = x[:n], x[n:], kf[:n], kf[n:]
    y = jnp.concatenate([xr * kr - xi * ki, xr * ki + xi * kr], axis=0).astype(BF16)
    o_ref[...] = jnp.dot(g_ref[...], y, preferred_element_type=F32)


def long_conv_direct(vx, kc):
    bsz, L, d = vx.shape
    n = 2 * L
    P = bsz // 2
    idx = jnp.arange(n, dtype=jnp.int32)
    fr, fi = _cis(idx[:, None] * idx[None, :], n, -1.0)
    f_data = _cblock(fr[:, :L], fi[:, :L]).astype(BF16)
    f_filt = jnp.concatenate([fr, fi], axis=0).astype(BF16)
    g = (_cblock(fr[:L, :], -fi[:L, :]) / n).astype(BF16)
    full = lambda a: pl.BlockSpec(a.shape, lambda p: (0,) * a.ndim)
    y = pl.pallas_call(
        functools.partial(_direct_conv_kernel, n=n),
        out_shape=jax.ShapeDtypeStruct((P, n, d), F32),
        grid=(P,),
        in_specs=[pl.BlockSpec((None, n, d), lambda p: (p, 0, 0)), full(kc), full(f_data), full(f_filt), full(g)],
        out_specs=pl.BlockSpec((None, n, d), lambda p: (p, 0, 0)),
        compiler_params=_params(1),
        name="conv_direct",
    )(vx.reshape(P, n, d), kc, f_data, f_filt, g)
    return y.reshape(bsz, L, d)


def _hy_tail_kernel(h_ref, x0_ref, vx_ref, cv_ref, mod_ref, fb_ref, w_ref, b_ref, o_ref):
    y = (x0_ref[...] * (cv_ref[...] + vx_ref[...] * fb_ref[...])).astype(BF16)
    o_ref[...] = h_ref[...] + mod_ref[5:6, :] * (jnp.dot(y, w_ref[...], preferred_element_type=F32) + b_ref[...])


def hy_tail(h, x0, vx, conv, mod, is_ctx, filt_bias, w_out, b_out):
    bsz, L, d = h.shape
    tm = min(HY_ROWS, L)
    row = (lambda b, i: (bsz, 0, 0)) if is_ctx else (lambda b, i: (b, 0, 0))
    tile = pl.BlockSpec((None, tm, d), lambda b, i: (b, i, 0))
    vec = pl.BlockSpec((1, d), lambda b, i: (0, 0))
    return pl.pallas_call(
        _hy_tail_kernel,
        out_shape=jax.ShapeDtypeStruct((bsz, L, d), F32),
        grid=(bsz, L // tm),
        in_specs=[tile, tile, tile, tile, pl.BlockSpec((None, N_MOD, d), row), vec,
                  pl.BlockSpec((d, d), lambda b, i: (0, 0)), vec],
        out_specs=tile,
        compiler_params=_params(2),
        name="hy_tail",
    )(h, x0, vx, conv, mod, filt_bias.reshape(1, d), w_out.astype(BF16), b_out.reshape(1, d))


def hyena_layer(h, mod, is_ctx, w_in, b_in, conv_w, conv_b, w1, b1, w2, b2, w3, b3, w4, sin_freq, filt_bias,
                w_out, b_out):
    L = h.shape[1]
    kc = hy_filter(L, w1, b1, w2, b2, w3, b3, w4, sin_freq)
    x0, vx = hy_in(h, mod, is_ctx, w_in, b_in, conv_w, conv_b)
    conv = long_conv_direct(vx, kc) if L <= FFT_DIRECT_MAX else long_conv_two_stage(vx, kc)
    return hy_tail(h, x0, vx, conv, mod, is_ctx, filt_bias, w_out, b_out)


def kernel(x, c, ctx, c_ctx, w_ada, b_ada, ffn_w_gate, ffn_w_up, ffn_w_down, hg_w_in, hg_lb_logits, hg_norm_gain,
           hg_w_out, pool_w, pool_scale, hy_w_in, hy_b_in, hy_conv_w, hy_conv_b, hy_w1, hy_b1, hy_w2, hy_b2, hy_w3,
           hy_b3, hy_w4, hy_sin_freq, hy_filt_bias, hy_w_out, hy_b_out, final_gain):
    depth = w_ada.shape[0]
    mods = ada_table(c, c_ctx, w_ada, b_ada)
    h, hc = x, ctx
    for i in range(depth):
        kind, j = i % 3, i // 3
        last = i == depth - 1
        ctx_live = (not last) or kind == 0
        mod = mods[i]
        ffn = lambda a, is_ctx, k0, s, gain=None: ffn_half_step(
            a, mod, is_ctx, k0, ffn_w_gate[i, s], ffn_w_up[i, s], ffn_w_down[i, s], gain)
        h = ffn(h, False, 0, 0)
        if ctx_live:
            hc = ffn(hc, True, 0, 0)
        if kind == 0:
            h, hc = hgrn2_layer(h, hc, mod, hg_w_in[j], hg_lb_logits, j, hg_norm_gain[j], hg_w_out[j], not last)
        elif kind == 1:
            h = pool_layer(h, mod, False, pool_w[j], pool_scale[j])
            if not last:
                hc = pool_layer(hc, mod, True, pool_w[j], pool_scale[j])
        else:
            hy = (hy_w_in[j], hy_b_in[j], hy_conv_w[j], hy_conv_b[j], hy_w1[j], hy_b1[j], hy_w2[j], hy_b2[j],
                  hy_w3[j], hy_b3[j], hy_w4[j], hy_sin_freq[j], hy_filt_bias[j], hy_w_out[j], hy_b_out[j])
            h = hyena_layer(h, mod, False, *hy)
            if not last:
                hc = hyena_layer(hc, mod, True, *hy)
        h = ffn(h, False, 6, 1, final_gain if last else None)
        if not last:
            hc = ffn(hc, True, 6, 1)
    return h
```

```python
import functools
import math

import jax
import jax.numpy as jnp
import numpy as np
from jax import lax
from jax.experimental import pallas as pl
from jax.experimental.pallas import tpu as pltpu

F32 = jnp.float32
BF16 = jnp.bfloat16

D_MODEL = 1024
DEPTH = 4
N_MOD = 9
D_FF = 2816
EPS = 1e-6
GRID_W = 64

HG_HEADS = 8
HG_DK = 128

POOL_WINDOWS = (2, 4, 8, 16)
POOL_GC = D_MODEL // len(POOL_WINDOWS)

HY_EMB = 33
HY_BANDS = (HY_EMB - 1) // 2
HY_ORDER = 64
HY_TARGET = 1e-2
HY_FAST_PCT = 0.3
HY_SLOW_PCT = 1.5

MOD_ROWS = 8
V7X_VMEM_LIMIT = 56 * 1024 * 1024


def _params(n_grid, vmem=None):
    return pltpu.CompilerParams(dimension_semantics=("arbitrary",) * n_grid,
                                vmem_limit_bytes=vmem or V7X_VMEM_LIMIT)


def _silu(x):
    return x * jax.nn.sigmoid(x)


def _rms(x):
    return x * lax.rsqrt(jnp.mean(x * x, axis=-1, keepdims=True) + EPS)


def _modulate(x, shift, scale):
    return _rms(x) * (1.0 + scale) + shift


def _ada_kernel(cc_ref, w_ref, b_ref, o_ref):
    s = _silu(cc_ref[...]).astype(BF16)
    o_ref[...] = jnp.dot(s, w_ref[...].astype(BF16), preferred_element_type=F32) + b_ref[...]


def ada_table(c, c_ctx, w_ada, b_ada):
    bsz, d = c.shape
    depth, _, nd = w_ada.shape
    cc = jnp.zeros((MOD_ROWS, d), F32).at[:bsz].set(c).at[bsz].set(c_ctx)
    tn = 1024
    out = pl.pallas_call(
        _ada_kernel,
        out_shape=jax.ShapeDtypeStruct((depth, MOD_ROWS, nd), F32),
        grid=(depth, nd // tn),
        in_specs=[pl.BlockSpec((MOD_ROWS, d), lambda l, j: (0, 0)),
                  pl.BlockSpec((None, d, tn), lambda l, j: (l, 0, j)),
                  pl.BlockSpec((None, 1, tn), lambda l, j: (l, 0, j))],
        out_specs=pl.BlockSpec((None, MOD_ROWS, tn), lambda l, j: (l, 0, j)),
        compiler_params=_params(2),
        name="ada_table",
    )(cc, w_ada, b_ada.reshape(depth, 1, nd))
    return out.reshape(depth, MOD_ROWS, N_MOD, d)


FFN_CHUNK = 256
FFN_ROWS = 512


def _ffn_kernel(h_ref, mod_ref, wg_ref, wu_ref, wd_ref, *rest, k0, n_chunks, final):
    if final:
        gain_ref, o_ref, acc_ref = rest
    else:
        o_ref, acc_ref = rest
    x = h_ref[...]
    y = _modulate(x, mod_ref[k0:k0 + 1, :], mod_ref[k0 + 1:k0 + 2, :]).astype(BF16)
    acc_ref[...] = jnp.zeros_like(acc_ref)

    def hidden(j):
        cols = slice(j * FFN_CHUNK, (j + 1) * FFN_CHUNK)
        g = jnp.dot(y, wg_ref[:, cols], preferred_element_type=F32)
        u = jnp.dot(y, wu_ref[:, cols], preferred_element_type=F32)
        return (_silu(g) * u).astype(BF16)

    a = hidden(0)
    for j in range(n_chunks):
        a_next = hidden(j + 1) if j + 1 < n_chunks else None
        acc_ref[...] += jnp.dot(a, wd_ref[j * FFN_CHUNK:(j + 1) * FFN_CHUNK, :], preferred_element_type=F32)
        a = a_next
    out = x + (0.5 * mod_ref[k0 + 2:k0 + 3, :]) * acc_ref[...]
    if final:
        out = _rms(out) * gain_ref[...]
    o_ref[...] = out


def ffn_half_step(h, mod, is_ctx, k0, w_gate, w_up, w_down, final_gain=None):
    bsz, L, d = h.shape
    f = w_gate.shape[1]
    n_chunks = f // FFN_CHUNK
    tm = min(FFN_ROWS, L)
    wg, wu, wd = w_gate.astype(BF16), w_up.astype(BF16), w_down.astype(BF16)
    row = (lambda b, i: (bsz, 0, 0)) if is_ctx else (lambda b, i: (b, 0, 0))
    const2 = lambda b, i: (0, 0)
    single = pl.Buffered(1)
    in_specs = [pl.BlockSpec((None, tm, d), lambda b, i: (b, i, 0)),
                pl.BlockSpec((None, N_MOD, d), row),
                pl.BlockSpec((d, f), const2, pipeline_mode=single),
                pl.BlockSpec((d, f), const2, pipeline_mode=single),
                pl.BlockSpec((f, d), const2, pipeline_mode=single)]
    args = [h, mod, wg, wu, wd]
    if final_gain is not None:
        in_specs.append(pl.BlockSpec((1, d), lambda b, i: (0, 0)))
        args.append(final_gain.reshape(1, d))
    return pl.pallas_call(
        functools.partial(_ffn_kernel, k0=k0, n_chunks=n_chunks, final=final_gain is not None),
        out_shape=jax.ShapeDtypeStruct((bsz, L, d), F32),
        grid=(bsz, L // tm),
        in_specs=in_specs,
        out_specs=pl.BlockSpec((None, tm, d), lambda b, i: (b, i, 0)),
        scratch_shapes=[pltpu.VMEM((tm, d), F32)],
        compiler_params=_params(2),
        name="ffn_half_step",
    )(*args)


HG_ROWS = 256
GLA_TILE = 256
GLA_BLOCK = 32
GLA_PAD_ROWS = 16
GLA_SAFE_LOG_DECAY = 80.0


def _split3(x):
    x1 = x.astype(BF16)
    r1 = x - x1.astype(F32)
    x2 = r1.astype(BF16)
    x3 = (r1 - x2.astype(F32)).astype(BF16)
    return x1, x2, x3


def _dot_exact_lhs(a_bf16, x):
    x1, x2, x3 = _split3(x)
    return (jnp.dot(a_bf16, x1, preferred_element_type=F32)
            + jnp.dot(a_bf16, x2, preferred_element_type=F32)
            + jnp.dot(a_bf16, x3, preferred_element_type=F32))


def _hg_proj_kernel(h_ref, mod_ref, w_ref, lbl_ref, q_ref, kf_ref, gf_ref, kb_ref, gb_ref, v_ref, sg_ref, *, layer_j):
    x = h_ref[...]
    u = _modulate(x, mod_ref[3:4, :], mod_ref[4:5, :]).astype(BF16)
    dm = x.shape[1]
    proj = lambda k: jnp.dot(u, w_ref[:, k * dm:(k + 1) * dm], preferred_element_type=F32)
    q_ref[...] = _silu(proj(0)).astype(q_ref.dtype)
    v_ref[...] = proj(3).astype(v_ref.dtype)
    sg_ref[...] = _silu(proj(4)).astype(sg_ref.dtype)
    for d, (k_ref, g_ref) in enumerate(((kf_ref, gf_ref), (kb_ref, gb_ref))):
        logits = lbl_ref[d]
        e = jnp.exp(logits - jnp.max(logits, axis=0, keepdims=True))
        p = e / jnp.sum(e, axis=0, keepdims=True)
        lb = jnp.sum(p[:layer_j + 1], axis=0, keepdims=True) - p[0:1]
        z = proj(1 + d)
        g_ref[...] = jnp.log(lb + (1.0 - lb) * jax.nn.sigmoid(z))
        k_ref[...] = (1.0 - lb) * jax.nn.sigmoid(-z)


def hg_project(h, mod, is_ctx, w_in, lb_logits, layer_j):
    bsz, L, d = h.shape
    tm = min(HG_ROWS, L)
    w = w_in.astype(BF16)
    row = (lambda b, i: (bsz, 0, 0)) if is_ctx else (lambda b, i: (b, 0, 0))
    tile = pl.BlockSpec((None, tm, d), lambda b, i: (b, i, 0))
    shp = jax.ShapeDtypeStruct((bsz, L, d), F32)
    half = jax.ShapeDtypeStruct((bsz, L, d), BF16)
    return pl.pallas_call(
        functools.partial(_hg_proj_kernel, layer_j=layer_j),
        out_shape=(half, shp, shp, shp, shp, half, half),
        grid=(bsz, L // tm),
        in_specs=[tile,
                  pl.BlockSpec((None, N_MOD, d), row),
                  pl.BlockSpec(w.shape, lambda b, i: (0, 0), pipeline_mode=pl.Buffered(1)),
                  pl.BlockSpec(lb_logits.shape, lambda b, i: (0, 0, 0))],
        out_specs=(tile,) * 7,
        compiler_params=_params(2),
        name="hg_project",
    )(h, mod, w, lb_logits)


def _gla_consts(tl, blk):
    r = np.arange(tl)
    same = (r[:, None] // blk) == (r[None, :] // blk)
    tri_f = same & (r[None, :] <= r[:, None])
    tri_b = same & (r[None, :] >= r[:, None])
    totals = np.zeros((GLA_PAD_ROWS, tl), bool)
    totals[np.arange(tl) // blk, np.arange(tl)] = True
    return jnp.asarray(np.stack([np.concatenate([tri_f, totals]), np.concatenate([tri_b, totals])]), BF16)


def _gla_kernel(qf_ref, qb_ref, kf_ref, gf_ref, kb_ref, gb_ref, vf_ref, vb_ref, sums_ref, s0_ref,
                of_ref, ob_ref, sfin_ref,
                st_ref, bc_ref, qd_ref, kt_ref, ke_ref, dec_ref, *, n_blk, blk, heads, dk):
    i = pl.program_id(1)

    @pl.when(i == 0)
    def _():
        st_ref[...] = s0_ref[...]

    dirs = ((qf_ref, kf_ref, gf_ref, vf_ref, of_ref), (qb_ref, kb_ref, gb_ref, vb_ref, ob_ref))
    worst = None
    for d, (q_ref, k_ref, g_ref, v_ref, _) in enumerate(dirs):
        g = g_ref[...]
        sums = _dot_exact_lhs(sums_ref[d], g)
        bc = sums[:n_blk * blk]
        tot = sums[n_blk * blk:n_blk * blk + n_blk]
        spread = lambda a: jnp.concatenate(
            [jnp.broadcast_to(a[j:j + 1], (blk, a.shape[1])) for j in range(n_blk)], axis=0)
        bl = spread(tot)
        bc_ref[d] = bc
        dec_ref[d] = spread(jnp.exp(tot))
        qd_ref[d] = (q_ref[...] * jnp.exp(bc)).astype(BF16)
        ke_ref[d] = (k_ref[...] * jnp.exp(bl - bc)).astype(BF16)
        m = jnp.min(tot)
        worst = m if worst is None else jnp.minimum(worst, m)
    safe = worst >= -GLA_SAFE_LOG_DECAY

    @pl.when(safe)
    def _():
        for d, (_, k_ref, _, _, _) in enumerate(dirs):
            kt_ref[d] = (k_ref[...] * jnp.exp(-bc_ref[d])).astype(BF16)

    t_io = lax.broadcasted_iota(jnp.int32, (blk, blk), 0)
    s_io = lax.broadcasted_iota(jnp.int32, (blk, blk), 1)
    t_io_w = lax.broadcasted_iota(jnp.int32, (blk, dk), 0)
    sub_io = lax.broadcasted_iota(jnp.int32, (8, dk), 0)
    nt =(((1,), (1,)), ((), ()))
    tn = (((0,), (0,)), ((), ()))

    def run(fast):
        def step(jb, carry):
            chains = []
            for d in range(2):
                r0 = pl.multiple_of((jb if d == 0 else n_blk - 1 - jb) * blk, blk)
                for hh in range(heads):
                    chains.append((d, hh, r0, pl.ds(r0, blk), slice(hh * dk, (hh + 1) * dk)))
            scores, carried = [], []
            for d, hh, r0, rows, cols in chains:
                q_ref, k_ref = dirs[d][0], dirs[d][1]
                qd = qd_ref[d, rows, cols]
                if fast:
                    sc = lax.dot_general(qd, kt_ref[d, rows, cols], nt, preferred_element_type=F32)
                    sc = jnp.where((s_io <= t_io) if d == 0 else (s_io >= t_io), sc, 0.0)
                else:
                    qq = q_ref[rows, cols]
                    bb = bc_ref[d, rows, cols]

                    def col(s, sc, d=d, r0=r0, cols=cols, qq=qq, bb=bb, k_ref=k_ref):
                        grp = pl.ds(pl.multiple_of(r0 + (s // 8) * 8, 8), 8)
                        pick = sub_io == s % 8
                        brow = jnp.sum(jnp.where(pick, bc_ref[d, grp, cols], 0.0), axis=0, keepdims=True)
                        krow = jnp.sum(jnp.where(pick, k_ref[grp, cols], 0.0), axis=0, keepdims=True)
                        ok = (t_io_w >= s) if d == 0 else (t_io_w <= s)
                        p = jnp.where(ok, qq * krow * jnp.exp(jnp.minimum(bb - brow, 0.0)), 0.0)
                        return jnp.where(s_io == s, jnp.sum(p, axis=-1, keepdims=True), sc)

                    sc = lax.fori_loop(0, blk, col, jnp.zeros((blk, blk), F32))
                scores.append(sc.astype(BF16))
                carried.append(lax.dot_general(qd, st_ref[d, hh].astype(BF16), nt, preferred_element_type=F32))
            for (d, hh, r0, rows, cols), sc, from_state in zip(chains, scores, carried):
                v_ref, o_ref = dirs[d][3], dirs[d][4]
                o = jnp.dot(sc, v_ref[rows, cols], preferred_element_type=F32) + from_state
                o_ref[rows, cols] = o.astype(o_ref.dtype)
            for d, hh, r0, rows, cols in chains:
                dec = dec_ref[d, pl.ds(r0, 1), cols]
                st_ref[d, hh] = st_ref[d, hh] * dec + lax.dot_general(
                    dirs[d][3][rows, cols], ke_ref[d, rows, cols], tn, preferred_element_type=F32)
            return carry

        lax.fori_loop(0, n_blk, step, 0, unroll=fast)

    @pl.when(safe)
    def _():
        run(True)

    @pl.when(jnp.logical_not(safe))
    def _():
        run(False)

    @pl.when(i == pl.num_programs(1) - 1)
    def _():
        sfin_ref[...] = st_ref[...]


def gla_bidirectional(q, k_f, g_f, k_b, g_b, v, s0):
    bsz, L, d = q.shape
    tl = min(GLA_TILE, L)
    nt = L // tl
    n_blk = tl // GLA_BLOCK
    sums = _gla_consts(tl, GLA_BLOCK)
    fwd = pl.BlockSpec((None, tl, d), lambda b, i: (b, i, 0))
    bwd = pl.BlockSpec((None, tl, d), lambda b, i: (b, nt - 1 - i, 0))
    st_spec = pl.BlockSpec((2, None, HG_HEADS, HG_DK, HG_DK), lambda b, i: (0, b, 0, 0, 0))
    shp = jax.ShapeDtypeStruct((bsz, L, d), BF16)
    return pl.pallas_call(
        functools.partial(_gla_kernel, n_blk=n_blk, blk=GLA_BLOCK, heads=HG_HEADS, dk=HG_DK),
        out_shape=(shp, shp, jax.ShapeDtypeStruct(s0.shape, F32)),
        grid=(bsz, nt),
        in_specs=[fwd, bwd, fwd, fwd, bwd, bwd, fwd, bwd,
                  pl.BlockSpec(sums.shape, lambda b, i: (0, 0, 0)),
                  st_spec],
        out_specs=(fwd, bwd, st_spec),
        scratch_shapes=[pltpu.VMEM((2, HG_HEADS, HG_DK, HG_DK), F32),
                        pltpu.VMEM((2, tl, d), F32),
                        pltpu.VMEM((2, tl, d), BF16),
                        pltpu.VMEM((2, tl, d), BF16),
                        pltpu.VMEM((2, tl, d), BF16),
                        pltpu.VMEM((2, tl, d), F32)],
        compiler_params=_params(2),
        name="gla_bidirectional",
    )(q, q, k_f, g_f, k_b, g_b, v, v, sums, s0)


def _hg_readout_kernel(h_ref, of_ref, ob_ref, sg_ref, mod_ref, gain_ref, w_ref, o_ref, *, heads, dk):
    o = of_ref[...].astype(F32) + ob_ref[...].astype(F32)
    parts = []
    for hh in range(heads):
        parts.append(_rms(o[:, hh * dk:(hh + 1) * dk]))
    on = jnp.concatenate(parts, axis=-1)
    y = (on * gain_ref[...] * sg_ref[...]).astype(BF16)
    o_ref[...] = h_ref[...] + mod_ref[5:6, :] * jnp.dot(y, w_ref[...], preferred_element_type=F32)


def hg_readout(h, o_f, o_b, sg, mod, is_ctx, norm_gain, w_out):
    bsz, L, d = h.shape
    tm = min(HG_ROWS, L)
    row = (lambda b, i: (bsz, 0, 0)) if is_ctx else (lambda b, i: (b, 0, 0))
    tile = pl.BlockSpec((None, tm, d), lambda b, i: (b, i, 0))
    return pl.pallas_call(
        functools.partial(_hg_readout_kernel, heads=HG_HEADS, dk=HG_DK),
        out_shape=jax.ShapeDtypeStruct((bsz, L, d), F32),
        grid=(bsz, L // tm),
        in_specs=[tile, tile, tile, tile,
                  pl.BlockSpec((None, N_MOD, d), row),
                  pl.BlockSpec((1, d), lambda b, i: (0, 0)),
                  pl.BlockSpec((d, d), lambda b, i: (0, 0))],
        out_specs=tile,
        compiler_params=_params(2),
        name="hg_readout",
    )(h, o_f, o_b, sg, mod, norm_gain.reshape(1, d), w_out.astype(BF16))


def hgrn2_layer(h, hc, mod, w_in, lb_logits, layer_j, norm_gain, w_out, ctx_out):
    bsz = h.shape[0]
    qc, kfc, gfc, kbc, gbc, vc, sgc = hg_project(hc, mod, True, w_in, lb_logits, layer_j)
    ql, kfl, gfl, kbl, gbl, vl, sgl = hg_project(h, mod, False, w_in, lb_logits, layer_j)
    s0 = jnp.zeros((2, bsz, HG_HEADS, HG_DK, HG_DK), F32)
    ocf, ocb, s_ctx = gla_bidirectional(qc, kfc, gfc, kbc, gbc, vc, s0)
    olf, olb, _ = gla_bidirectional(ql, kfl, gfl, kbl, gbl, vl, s_ctx)
    h = hg_readout(h, olf, olb, sgl, mod, False, norm_gain, w_out)
    if ctx_out:
        hc = hg_readout(hc, ocf, ocb, sgc, mod, True, norm_gain, w_out)
    return h, hc


POOL_ROWS = 256


def _window_matrix(n, w, period=None):
    p = np.arange(n)
    period = period or n
    lo, hi = p - w // 2, p + (w - w // 2)
    same = (p[:, None] // period) == (p[None, :] // period)
    return same & (p[None, :] >= lo[:, None]) & (p[None, :] < hi[:, None])


def _window_count(pos, n, w):
    return (jnp.clip(pos + (w - w // 2), 0, n) - jnp.clip(pos - w // 2, 0, n)).astype(F32)


def _pool_row_kernel(h_ref, mod_ref, a_ref, o_ref, *, n_rows):
    u = _modulate(h_ref[...], mod_ref[3:4, :], mod_ref[4:5, :])
    r = lax.broadcasted_iota(jnp.int32, (n_rows, 1), 0)
    for gi, w in enumerate(POOL_WINDOWS):
        cols = slice(gi * POOL_GC, (gi + 1) * POOL_GC)
        o_ref[:, cols] = _dot_exact_lhs(a_ref[gi], u[:, cols]) / _window_count(r, n_rows, w)


def _pool_main_kernel(*refs, period, on_grid):
    if on_grid:
        h_ref, r_ref, mod_ref, a_ref, wp_ref, sc_ref, o_ref = refs
    else:
        h_ref, mod_ref, a_ref, wp_ref, sc_ref, o_ref = refs
    x = h_ref[...]
    u = _modulate(x, mod_ref[3:4, :], mod_ref[4:5, :])
    src = r_ref[...] if on_grid else u
    pos = lax.broadcasted_iota(jnp.int32, (x.shape[0], 1), 0) % period
    for gi, w in enumerate(POOL_WINDOWS):
        cols = slice(gi * POOL_GC, (gi + 1) * POOL_GC)
        mean = _dot_exact_lhs(a_ref[gi], src[:, cols]) / _window_count(pos, period, w)
        pooled = (mean - u[:, cols]).astype(BF16)
        y = jnp.dot(pooled, wp_ref[gi], preferred_element_type=F32) * sc_ref[:, cols]
        o_ref[:, cols] = x[:, cols] + mod_ref[5:6, cols] * y


def pool_layer(h, mod, is_ctx, w_pool, scale):
    bsz, L, d = h.shape
    wp = w_pool.astype(BF16)
    row = (lambda b, i: (bsz, 0, 0)) if is_ctx else (lambda b, i: (b, 0, 0))
    mod_spec = pl.BlockSpec((None, N_MOD, d), row)
    tm = min(POOL_ROWS, L)
    tile = pl.BlockSpec((None, tm, d), lambda b, i: (b, i, 0))
    ng = len(POOL_WINDOWS)
    args, specs = [h], [tile]
    if is_ctx:
        assert tm == L, "the 1D windows need the whole context in one tile"
        period = L
    else:
        n_rows = L // GRID_W
        a_row = jnp.asarray(np.stack([_window_matrix(n_rows, w) for w in POOL_WINDOWS]), BF16)
        rows = pl.pallas_call(
            functools.partial(_pool_row_kernel, n_rows=n_rows),
            out_shape=jax.ShapeDtypeStruct((bsz, n_rows, GRID_W * d), F32),
            grid=(bsz, GRID_W),
            in_specs=[pl.BlockSpec((None, n_rows, d), lambda b, c: (b, 0, c)),
                      pl.BlockSpec((None, N_MOD, d), lambda b, c: (b, 0, 0)),
                      pl.BlockSpec((ng, n_rows, n_rows), lambda b, c: (0, 0, 0))],
            out_specs=pl.BlockSpec((None, n_rows, d), lambda b, c: (b, 0, c)),
            compiler_params=_params(2),
            name="pool_rows",
        )(h.reshape(bsz, n_rows, GRID_W * d), mod, a_row)
        args.append(rows.reshape(bsz, L, d))
        specs.append(tile)
        period = GRID_W
    a_col = jnp.asarray(np.stack([_window_matrix(tm, w, period) for w in POOL_WINDOWS]), BF16)
    args += [mod, a_col, wp, scale.reshape(1, d)]
    specs += [mod_spec,
              pl.BlockSpec((ng, tm, tm), lambda b, i: (0, 0, 0)),
              pl.BlockSpec((ng, POOL_GC, POOL_GC), lambda b, i: (0, 0, 0)),
              pl.BlockSpec((1, d), lambda b, i: (0, 0))]
    return pl.pallas_call(
        functools.partial(_pool_main_kernel, period=period, on_grid=not is_ctx),
        out_shape=jax.ShapeDtypeStruct((bsz, L, d), F32),
        grid=(bsz, L // tm),
        in_specs=specs,
        out_specs=tile,
        compiler_params=_params(2),
        name="pool_main",
    )(*args)


HY_ROWS = 256
HY_HALO = 8
FFT_N2 = 128
FFT_LANES = 2048
FFT_CH = 512
FFT_DIRECT_MAX = 512


def _dot_f32(a, b):
    a1, a2, a3 = _split3(a)
    b1, b2, b3 = _split3(b)
    d = lambda x, y: jnp.dot(x, y, preferred_element_type=F32)
    return d(a1, b1) + (d(a1, b2) + d(a2, b1)) + (d(a1, b3) + d(a2, b2) + d(a3, b1))


def _hy_in_kernel(h_ref, hp_ref, hn_ref, mod_ref, w_ref, b_ref, cw_ref, cb_ref, x0_ref, vx_ref, *, tm):
    i = pl.program_id(1)
    last = pl.num_programs(1) - 1
    sh, sc = mod_ref[3:4, :], mod_ref[4:5, :]
    u = jnp.concatenate([_modulate(hp_ref[...], sh, sc), _modulate(h_ref[...], sh, sc),
                         _modulate(hn_ref[...], sh, sc)], axis=0).astype(BF16)
    row = lax.broadcasted_iota(jnp.int32, (tm, 1), 0)
    outs = []
    for blk in range(3):
        dm = u.shape[1]
        zz = jnp.dot(u, w_ref[:, blk * dm:(blk + 1) * dm], preferred_element_type=F32) + b_ref[blk]
        z = zz[HY_HALO:HY_HALO + tm]
        before = jnp.where(i == 0, 0.0, zz[HY_HALO - 1:HY_HALO, :])
        after = jnp.where(i == last, 0.0, zz[HY_HALO + tm:HY_HALO + tm + 1, :])
        zm1 = jnp.where(row == 0, before, pltpu.roll(z, 1, axis=0))
        zp1 = jnp.where(row == tm - 1, after, pltpu.roll(z, tm - 1, axis=0))
        outs.append(zm1 * cw_ref[0, blk] + z * cw_ref[1, blk] + zp1 * cw_ref[2, blk] + cb_ref[blk])
    x0_ref[...] = outs[0]
    vx_ref[...] = outs[2] * outs[1]


def hy_in(h, mod, is_ctx, w_in, b_in, conv_w, conv_b):
    bsz, L, d = h.shape
    tm = min(HY_ROWS, L)
    per = tm // HY_HALO
    n_halo = L // HY_HALO
    row = (lambda b, i: (bsz, 0, 0)) if is_ctx else (lambda b, i: (b, 0, 0))
    tile = pl.BlockSpec((None, tm, d), lambda b, i: (b, i, 0))
    shp = jax.ShapeDtypeStruct((bsz, L, d), F32)
    const = lambda n: (lambda b, i: (0,) * n)
    return pl.pallas_call(
        functools.partial(_hy_in_kernel, tm=tm),
        out_shape=(shp, shp),
        grid=(bsz, L // tm),
        in_specs=[tile,
                  pl.BlockSpec((None, HY_HALO, d), lambda b, i: (b, jnp.maximum(i * per - 1, 0), 0)),
                  pl.BlockSpec((None, HY_HALO, d), lambda b, i: (b, jnp.minimum((i + 1) * per, n_halo - 1), 0)),
                  pl.BlockSpec((None, N_MOD, d), row),
                  pl.BlockSpec((d, 3 * d), const(2), pipeline_mode=pl.Buffered(1)),
                  pl.BlockSpec((3, 1, d), const(3)),
                  pl.BlockSpec((3, 3, 1, d), const(4)),
                  pl.BlockSpec((3, 1, d), const(3))],
        out_specs=(tile, tile),
        compiler_params=_params(2),
        name="hy_in",
    )(h, h, h, mod, w_in.astype(BF16), b_in.reshape(3, 1, d),
      conv_w.reshape(3, 3, 1, d), conv_b.reshape(3, 1, d))


def _hy_filter_kernel(band_ref, w1_ref, b1_ref, w2_ref, b2_ref, w3_ref, b3_ref, w4_ref, fr_ref, dl_ref, o_ref,
                      *, tr, L, d):
    n = pl.program_id(0) * tr + lax.broadcasted_iota(jnp.int32, (tr, 1), 0)
    j = jnp.where(n < L, n, 2 * L - n).astype(F32)
    t = j / (L - 1)
    lane = lax.broadcasted_iota(jnp.int32, (1, 128), 1)
    ang = (2 * math.pi * j / L) * band_ref[...]
    feats = jnp.where(lane == 0, t,
                      jnp.where(lane <= HY_BANDS, jnp.cos(ang),
                                jnp.where(lane <= 2 * HY_BANDS, -jnp.sin(ang), 0.0)))
    fr = fr_ref[...]
    a = jnp.sin(fr * (_dot_f32(feats, w1_ref[...]) + b1_ref[...]))
    a = jnp.sin(fr * (_dot_f32(a, w2_ref[...]) + b2_ref[...]))
    a = jnp.sin(fr * (_dot_f32(a, w3_ref[...]) + b3_ref[...]))
    kv = _dot_f32(a, w4_ref[...])
    val = jnp.where(n < L, kv[:, :d], kv[:, d:]) * jnp.exp(-t * dl_ref[...])
    o_ref[...] = jnp.where(n == L, 0.0, val)


def hy_filter(L, w1, b1, w2, b2, w3, b3, w4, sin_freq):
    d = w4.shape[1] // 2
    tr = 256
    band = np.zeros((1, 128), np.float32)
    bands = np.linspace(1e-4, HY_BANDS - 1, HY_BANDS, dtype=np.float32)
    band[0, 1:1 + HY_BANDS] = bands
    band[0, 1 + HY_BANDS:1 + 2 * HY_BANDS] = bands
    w1p = jnp.zeros((128, HY_ORDER), F32).at[:HY_EMB].set(w1)
    deltas = np.abs(np.linspace(math.log(HY_TARGET) / HY_SLOW_PCT, math.log(HY_TARGET) / HY_FAST_PCT, d,
                                dtype=np.float32)).reshape(1, d)
    full = lambda a: pl.BlockSpec(a.shape, lambda i: (0,) * a.ndim)
    args = [jnp.asarray(band), w1p, b1.reshape(1, -1), w2, b2.reshape(1, -1), w3, b3.reshape(1, -1), w4,
            sin_freq.reshape(1, -1), jnp.asarray(deltas)]
    return pl.pallas_call(
        functools.partial(_hy_filter_kernel, tr=tr, L=L, d=d),
        out_shape=jax.ShapeDtypeStruct((2 * L, d), F32),
        grid=(2 * L // tr,),
        in_specs=[full(a) for a in args],
        out_specs=pl.BlockSpec((tr, d), lambda i: (i, 0)),
        compiler_params=_params(1),
        name="hy_filter",
    )(*args)


def _cis(num, den, sign):
    ph = (num % den).astype(F32) * (2 * math.pi / den)
    return jnp.cos(ph), sign * jnp.sin(ph)


def _cblock(re, im):
    return jnp.concatenate([jnp.concatenate([re, -im], axis=-1), jnp.concatenate([im, re], axis=-1)], axis=-2)


def _left_kernel(m_ref, x_ref, o_ref):
    o_ref[...] = jnp.dot(m_ref[...], x_ref[...].astype(BF16), preferred_element_type=F32)


def fft_left(mat, x):
    P, K, lanes = x.shape
    M = mat.shape[0]
    nb = min(FFT_LANES, lanes)
    return pl.pallas_call(
        _left_kernel,
        out_shape=jax.ShapeDtypeStruct((P, M, lanes), F32),
        grid=(P, lanes // nb),
        in_specs=[pl.BlockSpec((M, K), lambda p, j: (0, 0)),
                  pl.BlockSpec((None, K, nb), lambda p, j: (p, 0, j))],
        out_specs=pl.BlockSpec((None, M, nb), lambda p, j: (p, 0, j)),
        compiler_params=_params(2),
        name="fft_left",
    )(mat.astype(BF16), x)


def _mid_kernel(*refs, pairs, n2, with_filter):
    if with_filter:
        a_ref, k_ref, mf_ref, mi_ref, o_ref = refs
    else:
        a_ref, mf_ref, o_ref = refs
    for p in range(pairs):
        a = jnp.concatenate([a_ref[p, 0], a_ref[p, 1]], axis=0).astype(BF16)
        x = jnp.dot(mf_ref[...], a, preferred_element_type=F32)
        xr, xi = x[:n2], x[n2:]
        if with_filter:
            kr, ki = k_ref[0], k_ref[1]
            y = jnp.concatenate([xr * kr - xi * ki, xr * ki + xi * kr], axis=0).astype(BF16)
            x = jnp.dot(mi_ref[...], y, preferred_element_type=F32)
            xr, xi = x[:n2], x[n2:]
        o_ref[p, 0] = xr
        o_ref[p, 1] = xi


def fft_mid(a, mf, kspec=None, mi=None):
    P, _, n1, n2, d = a.shape
    cb = min(FFT_CH, d)
    a_spec = pl.BlockSpec((P, 2, None, n2, cb), lambda c, f: (0, 0, f, 0, c))
    m_spec = pl.BlockSpec((None, 2 * n2, 2 * n2), lambda c, f: (f, 0, 0))
    if kspec is None:
        args, specs = [a, mf], [a_spec, m_spec]
    else:
        args = [a, kspec, mf, mi]
        specs = [a_spec, pl.BlockSpec((2, None, n2, cb), lambda c, f: (0, f, 0, c)), m_spec, m_spec]
    return pl.pallas_call(
        functools.partial(_mid_kernel, pairs=P, n2=n2, with_filter=kspec is not None),
        out_shape=jax.ShapeDtypeStruct(a.shape, F32),
        grid=(d // cb, n1),
        in_specs=specs,
        out_specs=a_spec,
        compiler_params=_params(2),
        name="fft_mid",
    )(*args)


def long_conv_two_stage(vx, kc):
    bsz, L, d = vx.shape
    n = 2 * L
    n2 = FFT_N2
    n1 = n // n2
    P = bsz // 2
    i1 = jnp.arange(n1, dtype=jnp.int32)
    i2 = jnp.arange(n2, dtype=jnp.int32)
    fr, fi = _cis(i1[:, None] * i1[None, :], n1, -1.0)
    ph = i2[None, None, :] * (i1[:, None, None] + n1 * i2[None, :, None])
    er, ei = _cis(ph, n, -1.0)
    mf = _cblock(er, ei).astype(BF16)
    mi = _cblock(er.transpose(0, 2, 1), -ei.transpose(0, 2, 1)).astype(BF16)
    gr, gi = _cis(i1[: n1 // 2, None] * i1[None, :], n1, 1.0)
    g = _cblock(gr, gi) / n
    ak = fft_left(jnp.concatenate([fr, fi], axis=0), kc.reshape(1, n1, n2 * d))
    kspec = fft_mid(ak.reshape(1, 2, n1, n2, d), mf)[0]
    a = fft_left(_cblock(fr[:, : n1 // 2], fi[:, : n1 // 2]), vx.reshape(P, n1, n2 * d))
    bmid = fft_mid(a.reshape(P, 2, n1, n2, d), mf, kspec, mi)
    y = fft_left(g, bmid.reshape(P, 2 * n1, n2 * d))
    return y.reshape(bsz, L, d)


def _direct_conv_kernel(x_ref, k_ref, f_ref, fk_ref, g_ref, o_ref, *, n):
    kf = jnp.dot(fk_ref[...], k_ref[...].astype(BF16), preferred_element_type=F32)
    x = jnp.dot(f_ref[...], x_ref[...].astype(BF16), preferred_element_type=F32)
    xr, xi, kr, ki = x[:n], x[n:], kf[:n], kf[n:]
    y = jnp.concatenate([xr * kr - xi * ki, xr * ki + xi * kr], axis=0).astype(BF16)
    o_ref[...] = jnp.dot(g_ref[...], y, preferred_element_type=F32)


def long_conv_direct(vx, kc):
    bsz, L, d = vx.shape
    n = 2 * L
    P = bsz // 2
    idx = jnp.arange(n, dtype=jnp.int32)
    fr, fi = _cis(idx[:, None] * idx[None, :], n, -1.0)
    f_data = _cblock(fr[:, :L], fi[:, :L]).astype(BF16)
    f_filt = jnp.concatenate([fr, fi], axis=0).astype(BF16)
    g = (_cblock(fr[:L, :], -fi[:L, :]) / n).astype(BF16)
    full = lambda a: pl.BlockSpec(a.shape, lambda p: (0,) * a.ndim)
    y = pl.pallas_call(
        functools.partial(_direct_conv_kernel, n=n),
        out_shape=jax.ShapeDtypeStruct((P, n, d), F32),
        grid=(P,),
        in_specs=[pl.BlockSpec((None, n, d), lambda p: (p, 0, 0)), full(kc), full(f_data), full(f_filt), full(g)],
        out_specs=pl.BlockSpec((None, n, d), lambda p: (p, 0, 0)),
        compiler_params=_params(1),
        name="conv_direct",
    )(vx.reshape(P, n, d), kc, f_data, f_filt, g)
    return y.reshape(bsz, L, d)


def _hy_tail_kernel(h_ref, x0_ref, vx_ref, cv_ref, mod_ref, fb_ref, w_ref, b_ref, o_ref):
    y = (x0_ref[...] * (cv_ref[...] + vx_ref[...] * fb_ref[...])).astype(BF16)
    o_ref[...] = h_ref[...] + mod_ref[5:6, :] * (jnp.dot(y, w_ref[...], preferred_element_type=F32) + b_ref[...])


def hy_tail(h, x0, vx, conv, mod, is_ctx, filt_bias, w_out, b_out):
    bsz, L, d = h.shape
    tm = min(HY_ROWS, L)
    row = (lambda b, i: (bsz, 0, 0)) if is_ctx else (lambda b, i: (b, 0, 0))
    tile = pl.BlockSpec((None, tm, d), lambda b, i: (b, i, 0))
    vec = pl.BlockSpec((1, d), lambda b, i: (0, 0))
    return pl.pallas_call(
        _hy_tail_kernel,
        out_shape=jax.ShapeDtypeStruct((bsz, L, d), F32),
        grid=(bsz, L // tm),
        in_specs=[tile, tile, tile, tile, pl.BlockSpec((None, N_MOD, d), row), vec,
                  pl.BlockSpec((d, d), lambda b, i: (0, 0)), vec],
        out_specs=tile,
        compiler_params=_params(2),
        name="hy_tail",
    )(h, x0, vx, conv, mod, filt_bias.reshape(1, d), w_out.astype(BF16), b_out.reshape(1, d))


def hyena_layer(h, mod, is_ctx, w_in, b_in, conv_w, conv_b, w1, b1, w2, b2, w3, b3, w4, sin_freq, filt_bias,
                w_out, b_out):
    L = h.shape[1]
    kc = hy_filter(L, w1, b1, w2, b2, w3, b3, w4, sin_freq)
    x0, vx = hy_in(h, mod, is_ctx, w_in, b_in, conv_w, conv_b)
    conv = long_conv_direct(vx, kc) if L <= FFT_DIRECT_MAX else long_conv_two_stage(vx, kc)
    return hy_tail(h, x0, vx, conv, mod, is_ctx, filt_bias, w_out, b_out)


def kernel(x, c, ctx, c_ctx, w_ada, b_ada, ffn_w_gate, ffn_w_up, ffn_w_down, hg_w_in, hg_lb_logits, hg_norm_gain,
           hg_w_out, pool_w, pool_scale, hy_w_in, hy_b_in, hy_conv_w, hy_conv_b, hy_w1, hy_b1, hy_w2, hy_b2, hy_w3,
           hy_b3, hy_w4, hy_sin_freq, hy_filt_bias, hy_w_out, hy_b_out, final_gain):
    depth = w_ada.shape[0]
    mods = ada_table(c, c_ctx, w_ada, b_ada)
    h, hc = x, ctx
    for i in range(depth):
        kind, j = i % 3, i // 3
        last = i == depth - 1
        ctx_live = (not last) or kind == 0
        mod = mods[i]
        ffn = lambda a, is_ctx, k0, s, gain=None: ffn_half_step(
            a, mod, is_ctx, k0, ffn_w_gate[i, s], ffn_w_up[i, s], ffn_w_down[i, s], gain)
        h = ffn(h, False, 0, 0)
        if ctx_live:
            hc = ffn(hc, True, 0, 0)
        if kind == 0:
            h, hc = hgrn2_layer(h, hc, mod, hg_w_in[j], hg_lb_logits, j, hg_norm_gain[j], hg_w_out[j], not last)
        elif kind == 1:
            h = pool_layer(h, mod, False, pool_w[j], pool_scale[j])
            if not last:
                hc = pool_layer(hc, mod, True, pool_w[j], pool_scale[j])
        else:
            hy = (hy_w_in[j], hy_b_in[j], hy_conv_w[j], hy_conv_b[j], hy_w1[j], hy_b1[j], hy_w2[j], hy_b2[j],
                  hy_w3[j], hy_b3[j], hy_w4[j], hy_sin_freq[j], hy_filt_bias[j], hy_w_out[j], hy_b_out[j])
            h = hyena_layer(h, mod, False, *hy)
            if not last:
                hc = hyena_layer(hc, mod, True, *hy)
        h = ffn(h, False, 6, 1, final_gain if last else None)
        if not last:
            hc = ffn(hc, True, 6, 1)
    return h
```
